```python
import jax, jax.numpy as jnp
from jax import lax
import numpy as np

D_MODEL = 2048
BATCH = 8
SEQ = 4096
DEPTH = 4
DEC_BATCH = 8
DEC_SEQ = 64
PAST_LEN = 1024

CHUNK = 64
N_MIXERS = 3
N_A = (DEPTH + 2) // 3
N_B = (DEPTH + 1) // 3
N_C = DEPTH // 3
FOX_HEADS = 16
FOX_HEAD_DIM = D_MODEL // FOX_HEADS
Q_BLOCK = 128
FORGET_BIAS = 3.0
HGRN_HEADS = 16
HGRN_KEY_DIM = D_MODEL // HGRN_HEADS
HGRN_VAL_DIM = D_MODEL // HGRN_HEADS
HGRN_BLOCK = 16
CONV_WIDTH = 31
D_FF = 4 * D_MODEL
PLE_DIM = 256
EPS = 1e-6
NEG_INF = -1e30

kernel_name = 'fox_hgrn2_conformer_stream_step'


def rmsnorm(x, g):
    xf = x.astype(jnp.float32)
    y = xf * lax.rsqrt(jnp.mean(xf * xf, axis=-1, keepdims=True) + EPS)
    return (y * g.astype(jnp.float32)).astype(x.dtype)


def layernorm(x, g, b):
    xf = x.astype(jnp.float32)
    mu = jnp.mean(xf, axis=-1, keepdims=True)
    var = jnp.mean(jnp.square(xf - mu), axis=-1, keepdims=True)
    y = (xf - mu) * lax.rsqrt(var + EPS) * g.astype(jnp.float32) + b.astype(jnp.float32)
    return y.astype(x.dtype)


def fox_qkv(hn, w_in, b_f):
    B, T, _ = hn.shape
    q, k, v, f_logit = jnp.split(hn @ w_in, [D_MODEL, 2 * D_MODEL, 3 * D_MODEL], axis=-1)
    shape = (B, T, FOX_HEADS, FOX_HEAD_DIM)
    logf = jax.nn.log_sigmoid((f_logit + b_f).astype(jnp.float32))
    return q.reshape(shape), k.reshape(shape), v.reshape(shape), logf


def fox_attend(q, k, v, cum_q, cum_k, pos_q, pos_k):
    s = jnp.einsum('bqhd,bkhd->bhqk', q, k).astype(jnp.float32) * (FOX_HEAD_DIM ** -0.5)
    s = s + cum_q[..., :, None] - cum_k[..., None, :]
    s = jnp.where(pos_k[None, :] <= pos_q[:, None], s, NEG_INF)
    p = jax.nn.softmax(s, axis=-1).astype(v.dtype)
    return jnp.einsum('bhqk,bkhd->bqhd', p, v)


def fox_prompt(hn, w_in, b_f, w_o):
    B, S, _ = hn.shape
    q, k, v, logf = fox_qkv(hn, w_in, b_f)
    cum = jnp.cumsum(logf, axis=1).transpose(0, 2, 1)
    nb = S // Q_BLOCK
    q_blocks = q.reshape(B, nb, Q_BLOCK, FOX_HEADS, FOX_HEAD_DIM).swapaxes(0, 1)
    c_blocks = cum.reshape(B, FOX_HEADS, nb, Q_BLOCK).transpose(2, 0, 1, 3)
    pos_k = jnp.arange(S)

    def one_block(args):
        qb, cb, bi = args
        pos_q = bi * Q_BLOCK + jnp.arange(Q_BLOCK)
        return fox_attend(qb, k, v, cb, cum, pos_q, pos_k)

    o = lax.map(one_block, (q_blocks, c_blocks, jnp.arange(nb)))
    o = o.swapaxes(0, 1).reshape(B, S, D_MODEL)
    return o @ w_o, k, v, logf


def fox_sample(hn, ck, cv, clogf, w_in, b_f, w_o):
    B, T, _ = hn.shape
    P = ck.shape[1]
    q, k, v, logf = fox_qkv(hn, w_in, b_f)
    k_all = jnp.concatenate([ck.astype(k.dtype), k], axis=1)
    v_all = jnp.concatenate([cv.astype(v.dtype), v], axis=1)
    cum = jnp.cumsum(jnp.concatenate([clogf.astype(jnp.float32), logf], axis=1), axis=1)
    cum = cum.transpose(0, 2, 1)
    o = fox_attend(q, k_all, v_all, cum[:, :, P:], cum, P + jnp.arange(T), jnp.arange(P + T))
    return o.reshape(B, T, D_MODEL) @ w_o, k, v, logf


def hgrn_scan(q, k, v, logf, s0):
    B, T = q.shape[:2]
    pad = (-T) % HGRN_BLOCK
    if pad:
        pw = ((0, 0), (0, pad), (0, 0), (0, 0))
        q, k, v, logf = (jnp.pad(a, pw) for a in (q, k, v, logf))
    n = (T + pad) // HGRN_BLOCK

    def blk(a):
        return a.reshape(B, n, HGRN_BLOCK, *a.shape[2:]).swapaxes(0, 1)

    qc, kc, vc, gc = blk(q), blk(k), blk(v), blk(logf)
    G = jnp.cumsum(gc, axis=2)
    G_last = G[:, :, -1:]
    q_t = qc * jnp.exp(G)
    k_t = kc * jnp.exp(-G)
    k_state = kc * jnp.exp(G_last - G)
    decay = jnp.exp(G_last[:, :, 0])
    tri = jnp.tril(jnp.ones((HGRN_BLOCK, HGRN_BLOCK), dtype=bool))
    A = jnp.where(tri, jnp.einsum('nbthc,nbshc->nbhts', q_t, k_t), 0.0)
    o_intra = jnp.einsum('nbhts,nbshv->nbthv', A, vc)

    def step(S, xs):
        q_i, ks_i, v_i, d_i = xs
        o = jnp.einsum('bthc,bhcv->bthv', q_i, S)
        S = S * d_i[..., None] + jnp.einsum('bthc,bthv->bhcv', ks_i, v_i)
        return S, o

    S_final, o_inter = lax.scan(step, s0, (q_t, k_state, vc, decay))
    o = (o_intra + o_inter).swapaxes(0, 1).reshape(B, n * HGRN_BLOCK, HGRN_HEADS, HGRN_VAL_DIM)
    return o[:, :T], S_final


def hgrn_mixer(hn, s0, w_in, lb, gnorm, w_o):
    B, T, _ = hn.shape
    q, f_logit, i_in, g = jnp.split(hn @ w_in, 4, axis=-1)
    f = lb + (1.0 - lb) * jax.nn.sigmoid(f_logit.astype(jnp.float32))
    hk = (B, T, HGRN_HEADS, HGRN_KEY_DIM)
    o, s_new = hgrn_scan(q.astype(jnp.float32).reshape(hk), (1.0 - f).reshape(hk),
                         i_in.astype(jnp.float32).reshape(B, T, HGRN_HEADS, HGRN_VAL_DIM),
                         jnp.log(f).reshape(hk), s0.astype(jnp.float32))
    o = rmsnorm(o, gnorm.reshape(HGRN_HEADS, HGRN_VAL_DIM)).reshape(B, T, D_MODEL)
    o = o * jax.nn.sigmoid(g.astype(jnp.float32))
    return o.astype(hn.dtype) @ w_o, s_new


def conv_module(hn, buf, w_pw1, b_pw1, w_dw, b_dw, ln_g, ln_b, w_pw2, b_pw2):
    a, gate = jnp.split(hn @ w_pw1 + b_pw1, 2, axis=-1)
    u = a * jax.nn.sigmoid(gate)
    ext = jnp.concatenate([buf.astype(u.dtype), u], axis=1)
    y = lax.conv_general_dilated(ext, w_dw[:, None, :].astype(u.dtype), window_strides=(1,),
                                 padding='VALID', dimension_numbers=('NWC', 'WIO', 'NWC'),
                                 feature_group_count=D_MODEL) + b_dw
    y = jax.nn.silu(layernorm(y, ln_g, ln_b))
    return y @ w_pw2 + b_pw2, ext[:, -(CONV_WIDTH - 1):]


def sq_relu_mlp(hn, w_up, w_down):
    return jnp.square(jax.nn.relu(hn @ w_up)) @ w_down


def ple_add(h, p, w_ple, w_pg, g):
    return h + (p @ w_ple) * jax.nn.sigmoid(rmsnorm(h, g) @ w_pg)


def setup_inputs(seed: int = 0) -> dict:
    key = jax.random.key(seed)
    keys = iter(jax.random.split(key, 48))

    def nrm(shape, scale=1.0):
        return jax.random.normal(next(keys), shape, jnp.float32) * scale

    D = D_MODEL
    return {
        'x_prompt': nrm((BATCH, SEQ, D)),
        'x_sample': nrm((DEC_BATCH, DEC_SEQ, D)),
        'p_prompt': nrm((DEPTH, BATCH, SEQ, PLE_DIM)),
        'p_sample': nrm((DEPTH, DEC_BATCH, DEC_SEQ, PLE_DIM)),
        'cache_fox_k': nrm((N_A, DEC_BATCH, PAST_LEN, FOX_HEADS, FOX_HEAD_DIM)),
        'cache_fox_v': nrm((N_A, DEC_BATCH, PAST_LEN, FOX_HEADS, FOX_HEAD_DIM)),
        'cache_fox_logf': jax.nn.log_sigmoid(FORGET_BIAS + nrm((N_A, DEC_BATCH, PAST_LEN, FOX_HEADS))),
        'state_hgrn': nrm((N_B, DEC_BATCH, HGRN_HEADS, HGRN_KEY_DIM, HGRN_VAL_DIM), 0.5),
        'cache_conv': nrm((N_C, DEC_BATCH, CONV_WIDTH - 1, D)),
        'norm_mix': 1.0 + nrm((DEPTH, D), 0.01),
        'norm_ffn': 1.0 + nrm((DEPTH, D), 0.01),
        'norm_ple': 1.0 + nrm((DEPTH, D), 0.01),
        'norm_final': 1.0 + nrm((D,), 0.01),
        'w_in_a': nrm((N_A, D, 3 * D + FOX_HEADS), D ** -0.5),
        'b_f_a': FORGET_BIAS + nrm((N_A, FOX_HEADS), 0.1),
        'w_o_a': nrm((N_A, D, D), D ** -0.5),
        'w_in_b': nrm((N_B, D, 4 * D), D ** -0.5),
        'lb_logits': nrm((DEPTH, D), 0.1),
        'gnorm_b': 1.0 + nrm((N_B, D), 0.01),
        'w_o_b': nrm((N_B, D, D), D ** -0.5),
        'w_pw1': nrm((N_C, D, 2 * D), D ** -0.5),
        'b_pw1': nrm((N_C, 2 * D), 0.01),
        'w_dw': nrm((N_C, CONV_WIDTH, D), CONV_WIDTH ** -0.5),
        'b_dw': nrm((N_C, D), 0.01),
        'ln_g': 1.0 + nrm((N_C, D), 0.01),
        'ln_b': nrm((N_C, D), 0.01),
        'w_pw2': nrm((N_C, D, D), D ** -0.5),
        'b_pw2': nrm((N_C, D), 0.01),
        'w_up': nrm((DEPTH, D, D_FF), D ** -0.5),
        'w_down': nrm((DEPTH, D_FF, D), D_FF ** -0.5),
        'w_ple': nrm((DEPTH, PLE_DIM, D), PLE_DIM ** -0.5),
        'w_pg': nrm((DEPTH, D, D), D ** -0.5),
    }


def reference(x_prompt, x_sample, p_prompt, p_sample,
              cache_fox_k, cache_fox_v, cache_fox_logf, state_hgrn, cache_conv,
              norm_mix, norm_ffn, norm_ple, norm_final,
              w_in_a, b_f_a, w_o_a,
              w_in_b, lb_logits, gnorm_b, w_o_b,
              w_pw1, b_pw1, w_dw, b_dw, ln_g, ln_b, w_pw2, b_pw2,
              w_up, w_down, w_ple, w_pg):
    lb_soft = jax.nn.softmax(lb_logits.astype(jnp.float32), axis=0)
    lower_bounds = jnp.cumsum(lb_soft, axis=0) - lb_soft[0]

    hp, hs = x_prompt, x_sample
    Bp, Bs = x_prompt.shape[0], x_sample.shape[0]
    fkp, fvp, flp, fks, fvs, fls = [], [], [], [], [], []
    hsp, hss, cbp, cbs = [], [], [], []
    ia = ib = ic = 0
    for i in range(DEPTH):
        kind = i % N_MIXERS
        hn_p = rmsnorm(hp, norm_mix[i])
        hn_s = rmsnorm(hs, norm_mix[i])
        if kind == 0:
            mp, kp, vp, lp = fox_prompt(hn_p, w_in_a[ia], b_f_a[ia], w_o_a[ia])
            ms, ks_, vs, ls = fox_sample(hn_s, cache_fox_k[ia], cache_fox_v[ia], cache_fox_logf[ia],
                                         w_in_a[ia], b_f_a[ia], w_o_a[ia])
            fkp.append(kp); fvp.append(vp); flp.append(lp)
            fks.append(ks_); fvs.append(vs); fls.append(ls)
            ia += 1
        elif kind == 1:
            s0 = jnp.zeros((Bp, HGRN_HEADS, HGRN_KEY_DIM, HGRN_VAL_DIM), jnp.float32)
            mp, sp = hgrn_mixer(hn_p, s0, w_in_b[ib], lower_bounds[i], gnorm_b[ib], w_o_b[ib])
            ms, ss = hgrn_mixer(hn_s, state_hgrn[ib], w_in_b[ib], lower_bounds[i], gnorm_b[ib], w_o_b[ib])
            hsp.append(sp); hss.append(ss)
            ib += 1
        else:
            buf0 = jnp.zeros((Bp, CONV_WIDTH - 1, D_MODEL), hn_p.dtype)
            mp, bp = conv_module(hn_p, buf0, w_pw1[ic], b_pw1[ic], w_dw[ic], b_dw[ic],
                                 ln_g[ic], ln_b[ic], w_pw2[ic], b_pw2[ic])
            ms, bs = conv_module(hn_s, cache_conv[ic], w_pw1[ic], b_pw1[ic], w_dw[ic], b_dw[ic],
                                 ln_g[ic], ln_b[ic], w_pw2[ic], b_pw2[ic])
            cbp.append(bp); cbs.append(bs)
            ic += 1
        hp = hp + mp
        hs = hs + ms
        hp = hp + sq_relu_mlp(rmsnorm(hp, norm_ffn[i]), w_up[i], w_down[i])
        hs = hs + sq_relu_mlp(rmsnorm(hs, norm_ffn[i]), w_up[i], w_down[i])
        hp = ple_add(hp, p_prompt[i], w_ple[i], w_pg[i], norm_ple[i])
        hs = ple_add(hs, p_sample[i], w_ple[i], w_pg[i], norm_ple[i])

    y_prompt = rmsnorm(hp, norm_final)
    y_sample = rmsnorm(hs, norm_final)
    return (y_prompt, y_sample,
            jnp.stack(fkp), jnp.stack(fvp), jnp.stack(flp),
            jnp.stack(fks), jnp.stack(fvs), jnp.stack(fls),
            jnp.stack(hsp), jnp.stack(hss),
            jnp.stack(cbp), jnp.stack(cbs))
```

```python
import functools

import jax
import jax.numpy as jnp
from jax import lax
from jax.experimental import pallas as pl
from jax.experimental.pallas import tpu as pltpu

F32 = jnp.float32
BF16 = jnp.bfloat16
EPS = 1e-6
NEG_INF = -1e30
HGRN_BLOCK = 16
LANES = 128
SUBLANES = 8
VMEM_LIMIT_BYTES = 56 * 1024 * 1024

_NT = (((1,), (1,)), ((), ()))
_TN = (((0,), (0,)), ((), ()))


def _tile(n, pref, mult=SUBLANES):
    for d in range(min(n, pref), 0, -1):
        if n % d == 0 and d % mult == 0:
            return d
    return n


def _params(*sem):
    return pltpu.CompilerParams(dimension_semantics=sem, vmem_limit_bytes=VMEM_LIMIT_BYTES)


def _rms(x, g):
    return x * lax.rsqrt(jnp.mean(x * x, axis=-1, keepdims=True) + EPS) * g


def _split3(x):
    hi = x.astype(BF16)
    r1 = x - hi.astype(F32)
    mid = r1.astype(BF16)
    lo = (r1 - mid.astype(F32)).astype(BF16)
    return hi, mid, lo


def _dot_ones(mask_bf16, x):
    out = None
    for part in _split3(x):
        d = jnp.dot(mask_bf16, part, preferred_element_type=F32)
        out = d if out is None else out + d
    return out


def _dot_ones_rhs(x, mask_bf16):
    out = None
    for part in _split3(x):
        d = jnp.dot(part, mask_bf16, preferred_element_type=F32)
        out = d if out is None else out + d
    return out


def _proj(x, w, *, tm, wt=None, gamma=None, bias=None, head_major_x=False,
          extras=(), outs, epilogue, name):
    K, N = w.shape
    wt = N if wt is None else wt
    if head_major_x:
        B, H, T, dh = x.shape
        M = B * T
        n_t = T // tm
        x_spec = pl.BlockSpec((None, H, tm, dh), lambda i, j: (i // n_t, 0, i % n_t, 0))
    else:
        M = x.shape[0]
        x_spec = pl.BlockSpec((tm, K), lambda i, j: (i, 0))
    has_g = gamma is not None
    has_b = bias is not None
    assert M % tm == 0 and N % wt == 0
    assert has_g == (x.dtype == F32)

    def kernel(*refs):
        it = iter(refs)
        x_ref = next(it)
        g_ref = next(it) if has_g else None
        w_ref = next(it)
        b_ref = next(it) if has_b else None
        extra_refs = [next(it) for _ in extras]
        out_refs = [next(it) for _ in outs]
        xn_ref = next(it) if has_g else None
        j = pl.program_id(1)
        if has_g:
            @pl.when(j == 0)
            def _():
                xn_ref[...] = _rms(x_ref[...], g_ref[...]).astype(BF16)
            lhs = xn_ref[...]
        elif head_major_x:
            lhs = jnp.concatenate([x_ref[h] for h in range(x_ref.shape[0])], axis=1)
        else:
            lhs = x_ref[...]
        acc = jnp.dot(lhs, w_ref[...], preferred_element_type=F32)
        if has_b:
            acc = acc + b_ref[...]
        epilogue(j, acc, dict(x=x_ref, lhs=lhs, extras=extra_refs, outs=out_refs))

    in_arrays = [x]
    in_specs = [x_spec]
    if has_g:
        in_arrays.append(gamma.reshape(1, K).astype(F32))
        in_specs.append(pl.BlockSpec((1, K), lambda i, j: (0, 0)))
    in_arrays.append(w)
    in_specs.append(pl.BlockSpec((K, wt), lambda i, j: (0, j)))
    if has_b:
        in_arrays.append(bias.reshape(1, N).astype(F32))
        in_specs.append(pl.BlockSpec((1, wt), lambda i, j: (0, j)))
    for arr, blk, imap in extras:
        in_arrays.append(arr)
        in_specs.append(pl.BlockSpec(blk, imap))
    return pl.pallas_call(
        kernel,
        grid=(M // tm, N // wt),
        in_specs=in_specs,
        out_specs=[pl.BlockSpec(blk, imap) for _, blk, imap in outs],
        out_shape=[sds for sds, _, _ in outs],
        scratch_shapes=[pltpu.VMEM((tm, K), BF16)] if has_g else [],
        compiler_params=_params("arbitrary", "arbitrary"),
        name=name,
    )(*in_arrays)


def _store_head_major(ref, acc):
    for h in range(ref.shape[0]):
        ref[h] = acc[:, h * ref.shape[2]:(h + 1) * ref.shape[2]].astype(ref.dtype)


def _head_major_out(B, H, T, dh, tm):
    n_t = T // tm
    return (jax.ShapeDtypeStruct((B, H, T, dh), BF16), (None, H, tm, dh),
            lambda i, j: (i // n_t, 0, i % n_t, 0))


def _row_out(M, N, tm, dtype, wt=None):
    wt = N if wt is None else wt
    return (jax.ShapeDtypeStruct((M, N), dtype), (tm, wt), lambda i, j: (i, j))


def _log_sigmoid(x):
    return jnp.minimum(x, 0.0) - jnp.log(1.0 + jnp.exp(-jnp.abs(x)))


def _fox_project(h, gamma, wq, wk, wv, wf, bf, *, B, T, H):
    M, D = h.shape
    dh = D // H
    tm = _tile(T, 512)
    scale = dh ** -0.5

    def q_epilogue(j, acc, ctx):
        q_ref, logf_ref = ctx["outs"]
        wf_ref, bf_ref = ctx["extras"]
        _store_head_major(q_ref, acc * scale)
        fl = jnp.dot(ctx["lhs"], wf_ref[...], preferred_element_type=F32)
        logf_ref[...] = _log_sigmoid(fl[:, :H] + bf_ref[...])

    wf_pad = jnp.pad(wf, ((0, 0), (0, LANES - H)))
    q_hm, logf = _proj(
        h, wq, tm=tm, gamma=gamma,
        extras=[(wf_pad, (D, LANES), lambda i, j: (0, 0)),
                (bf.reshape(1, H).astype(F32), (1, H), lambda i, j: (0, 0))],
        outs=[_head_major_out(B, H, T, dh, tm), _row_out(M, H, tm, F32)],
        epilogue=q_epilogue, name="fox_q_proj")

    def kv_epilogue(j, acc, ctx):
        full_ref, hm_ref = ctx["outs"]
        full_ref[...] = acc
        _store_head_major(hm_ref, acc)

    kv = []
    for w, nm in ((wk, "fox_k_proj"), (wv, "fox_v_proj")):
        kv.append(_proj(h, w, tm=tm, gamma=gamma,
                        outs=[_row_out(M, D, tm, F32), _head_major_out(B, H, T, dh, tm)],
                        epilogue=kv_epilogue, name=nm))
    (k32, k_hm), (v32, v_hm) = kv
    return q_hm, k32, k_hm, v32, v_hm, logf


def _cumsum_lanes(x):
    R, L = x.shape
    assert L % LANES == 0

    def kernel(x_ref, o_ref):
        upper = (lax.broadcasted_iota(jnp.int32, (LANES, LANES), 0)
                 <= lax.broadcasted_iota(jnp.int32, (LANES, LANES), 1)).astype(BF16)
        carry = jnp.zeros((R, 1), F32)
        for c in range(L // LANES):
            sl = slice(c * LANES, (c + 1) * LANES)
            loc = _dot_ones_rhs(x_ref[:, sl], upper)
            o_ref[:, sl] = loc + carry
            carry = carry + loc[:, LANES - 1:LANES]

    return pl.pallas_call(
        kernel,
        out_shape=jax.ShapeDtypeStruct((R, L), F32),
        compiler_params=pltpu.CompilerParams(vmem_limit_bytes=VMEM_LIMIT_BYTES),
        name="fox_cumsum",
    )(x)


def _fox_attention(q, k, v, cq, ck, *, q_off, tq, tk):
    B, H, Tq, dh = q.shape
    Tk = k.shape[2]
    nq, nk = Tq // tq, Tk // tk
    assert Tq % tq == 0 and Tk % tk == 0

    def last_k(qi):
        return jnp.minimum(nk - 1, (q_off + (qi + 1) * tq - 1) // tk)

    def kernel(q_ref, k_ref, v_ref, cq_ref, ck_ref, o_ref, m_ref, l_ref, acc_ref, cqs_ref):
        qi = pl.program_id(1)
        ki = pl.program_id(2)

        @pl.when(ki == 0)
        def _():
            m_ref[...] = jnp.full(m_ref.shape, NEG_INF, F32)
            l_ref[...] = jnp.zeros(l_ref.shape, F32)
            acc_ref[...] = jnp.zeros(acc_ref.shape, F32)
            cqv = cq_ref[...]
            for h in range(H):
                cqs_ref[h] = cqv[:, h:h + 1]

        q_lo = q_off + qi * tq
        k_lo = ki * tk
        active = k_lo <= q_lo + tq - 1
        crosses = k_lo + tk - 1 > q_lo

        def heads(masked):
            def body(h, carry):
                s = lax.dot_general(q_ref[h], k_ref[h], _NT, preferred_element_type=F32)
                s = s + cqs_ref[h] - ck_ref[pl.ds(h, 1), :]
                if masked:
                    rel = (lax.broadcasted_iota(jnp.int32, (tq, tk), 1)
                           - lax.broadcasted_iota(jnp.int32, (tq, tk), 0))
                    s = jnp.where(rel <= q_lo - k_lo, s, NEG_INF)
                m_prev = m_ref[h]
                m_new = jnp.maximum(m_prev, jnp.max(s, axis=1, keepdims=True))
                alpha = jnp.exp(m_prev - m_new)
                p = jnp.exp(s - m_new)
                l_ref[h] = alpha * l_ref[h] + jnp.sum(p, axis=1, keepdims=True)
                acc_ref[h] = alpha * acc_ref[h] + jnp.dot(
                    p.astype(BF16), v_ref[h], preferred_element_type=F32)
                m_ref[h] = m_new
                return carry
            lax.fori_loop(0, H, body, 0)

        @pl.when(jnp.logical_and(active, crosses))
        def _():
            heads(True)

        @pl.when(jnp.logical_and(active, jnp.logical_not(crosses)))
        def _():
            heads(False)

        @pl.when(ki == nk - 1)
        def _():
            o_ref[...] = (acc_ref[...] / l_ref[...]).astype(o_ref.dtype)

    kv_spec = pl.BlockSpec((None, H, tk, dh),
                           lambda b, qi, ki: (b, 0, jnp.minimum(ki, last_k(qi)), 0))
    return pl.pallas_call(
        kernel,
        grid=(B, nq, nk),
        in_specs=[
            pl.BlockSpec((None, H, tq, dh), lambda b, qi, ki: (b, 0, qi, 0)),
            kv_spec, kv_spec,
            pl.BlockSpec((None, tq, H), lambda b, qi, ki: (b, qi, 0)),
            pl.BlockSpec((None, H, tk), lambda b, qi, ki: (b, 0, jnp.minimum(ki, last_k(qi)))),
        ],
        out_specs=pl.BlockSpec((None, H, tq, dh), lambda b, qi, ki: (b, 0, qi, 0)),
        out_shape=jax.ShapeDtypeStruct((B, H, Tq, dh), BF16),
        scratch_shapes=[pltpu.VMEM((H, tq, 1), F32), pltpu.VMEM((H, tq, 1), F32),
                        pltpu.VMEM((H, tq, dh), F32), pltpu.VMEM((H, tq, 1), F32)],
        compiler_params=_params("arbitrary", "arbitrary", "arbitrary"),
        name="fox_attention",
    )(q, k, v, cq, ck)


def _to_head_major(x):
    return jnp.transpose(x, (0, 2, 1, 3)).astype(BF16)


def _fox_layer(h, gamma, wts, *, B, T, cache=None):
    wq, wk, wv, wf, bf, wo = wts
    M, D = h.shape
    H = bf.shape[0]
    dh = D // H
    q_hm, k32, k_hm, v32, v_hm, logf = _fox_project(h, gamma, wq, wk, wv, wf, bf, B=B, T=T, H=H)
    logf_bth = logf.reshape(B, T, H)
    if cache is None:
        P = 0
        logf_all = logf_bth
    else:
        ck_, cv_, cl_ = cache
        P = ck_.shape[1]
        k_hm = jnp.concatenate([_to_head_major(ck_), k_hm], axis=2)
        v_hm = jnp.concatenate([_to_head_major(cv_), v_hm], axis=2)
        logf_all = jnp.concatenate([cl_.astype(F32), logf_bth], axis=1)
    Tk = P + T
    tk = _tile(Tk, 512, LANES) if Tk % LANES == 0 else -(-Tk // LANES) * LANES
    pad = -Tk % tk
    if pad:
        k_hm = jnp.pad(k_hm, ((0, 0), (0, 0), (0, pad), (0, 0)))
        v_hm = jnp.pad(v_hm, ((0, 0), (0, 0), (0, pad), (0, 0)))
        logf_all = jnp.pad(logf_all, ((0, 0), (0, pad), (0, 0)))
    rows = jnp.transpose(logf_all, (0, 2, 1)).reshape(B * H, Tk + pad)
    ck = _cumsum_lanes(rows).reshape(B, H, Tk + pad)
    cq = jnp.transpose(ck[:, :, P:P + T], (0, 2, 1))
    o_hm = _fox_attention(q_hm, k_hm, v_hm, cq, ck, q_off=P, tq=_tile(T, 512), tk=tk)
    h = _out_proj(o_hm, wo, h, T=T, head_major=True, name="fox_out_proj")
    return h, k32.reshape(B, T, H, dh), v32.reshape(B, T, H, dh), logf_bth


def _out_proj(x, w, h, *, T, bias=None, head_major=False, name):
    M, D = h.shape
    tm = _tile(T, 512)

    def epilogue(j, acc, ctx):
        ctx["outs"][0][...] = ctx["extras"][0][...] + acc

    return _proj(x, w, tm=tm, bias=bias, head_major_x=head_major,
                 extras=[(h, (tm, D), lambda i, j: (i, 0))],
                 outs=[_row_out(M, D, tm, F32)], epilogue=epilogue, name=name)[0]


def _ffn(h, gamma, w_up, w_down, *, T):
    M, D = h.shape
    F = w_up.shape[1]
    tm = _tile(T, 512)
    tf = _tile(F, 1024, LANES)

    def kernel(x_ref, g_ref, wu_ref, wd_ref, o_ref, xn_ref):
        @pl.when(pl.program_id(1) == 0)
        def _():
            x = x_ref[...]
            xn_ref[...] = _rms(x, g_ref[...]).astype(BF16)
            o_ref[...] = x
        hid = jnp.dot(xn_ref[...], wu_ref[...], preferred_element_type=F32)
        hid = jnp.square(jnp.maximum(hid, 0.0)).astype(BF16)
        o_ref[...] += jnp.dot(hid, wd_ref[...], preferred_element_type=F32)

    return pl.pallas_call(
        kernel,
        grid=(M // tm, F // tf),
        in_specs=[pl.BlockSpec((tm, D), lambda i, f: (i, 0)),
                  pl.BlockSpec((1, D), lambda i, f: (0, 0)),
                  pl.BlockSpec((D, tf), lambda i, f: (0, f)),
                  pl.BlockSpec((tf, D), lambda i, f: (f, 0))],
        out_specs=pl.BlockSpec((tm, D), lambda i, f: (i, 0)),
        out_shape=jax.ShapeDtypeStruct((M, D), F32),
        scratch_shapes=[pltpu.VMEM((tm, D), BF16)],
        compiler_params=_params("arbitrary", "arbitrary"),
        name="sq_relu_mlp",
    )(h, gamma.reshape(1, D).astype(F32), w_up, w_down)


def _ple(h, p, gamma, w_ple, w_pg, *, T, final_gamma=None):
    M, D = h.shape
    E = p.shape[1]
    tm = _tile(T, 256)
    fin = final_gamma is not None

    def epilogue(j, acc, ctx):
        p_ref, wple_ref = ctx["extras"][:2]
        e = jnp.dot(p_ref[...].astype(BF16), wple_ref[...], preferred_element_type=F32)
        out = ctx["x"][...] + e * jax.nn.sigmoid(acc)
        if fin:
            out = _rms(out, ctx["extras"][2][...])
        ctx["outs"][0][...] = out

    extras = [(p, (tm, E), lambda i, j: (i, 0)), (w_ple, (E, D), lambda i, j: (0, 0))]
    if fin:
        extras.append((final_gamma.reshape(1, D).astype(F32), (1, D), lambda i, j: (0, 0)))
    return _proj(h, w_pg, tm=tm, gamma=gamma, extras=extras,
                 outs=[_row_out(M, D, tm, F32)], epilogue=epilogue, name="ple_add")[0]


def _hgrn_scan(q, fl, iv, g, lb, gn, s0, *, B, T, tc):
    M, D = q.shape
    H, dk, dv = s0.shape[1:]
    assert dk == LANES and dv == LANES and T % tc == 0 and tc % HGRN_BLOCK == 0
    n_t = T // tc
    n_sub = tc // HGRN_BLOCK

    def kernel(q_ref, fl_ref, iv_ref, g_ref, lb_ref, gn_ref, s0_ref, o_ref, sf_ref,
               st_ref, qt_ref, kst_ref, dec_ref, oi_ref):
        t = pl.program_id(2)

        @pl.when(t == 0)
        def _():
            st_ref[...] = s0_ref[...].T

        lbv = lb_ref[...]
        f = lbv + (1.0 - lbv) * jax.nn.sigmoid(fl_ref[...])
        logf = jnp.log(f)
        kk = 1.0 - f
        r = lax.broadcasted_iota(jnp.int32, (tc, tc), 0)
        c = lax.broadcasted_iota(jnp.int32, (tc, tc), 1)
        shift = HGRN_BLOCK.bit_length() - 1
        same = jnp.right_shift(r, shift) == jnp.right_shift(c, shift)
        causal = jnp.logical_and(same, c <= r)
        G = _dot_ones(jnp.where(causal, 1.0, 0.0).astype(BF16), logf)
        GL = _dot_ones(jnp.where(same, 1.0, 0.0).astype(BF16), logf)
        qt = (q_ref[...] * jnp.exp(G)).astype(BF16)
        kt = (kk * jnp.exp(-G)).astype(BF16)
        a = lax.dot_general(qt, kt, _NT, preferred_element_type=F32)
        a = jnp.where(causal, a, 0.0).astype(BF16)
        oi_ref[...] = jnp.dot(a, iv_ref[...], preferred_element_type=F32)
        qt_ref[...] = qt
        kst_ref[...] = (kk * jnp.exp(GL - G)).astype(BF16)
        dec_ref[...] = jnp.exp(GL)

        def sub(n, carry):
            rows = pl.ds(pl.multiple_of(n * HGRN_BLOCK, HGRN_BLOCK), HGRN_BLOCK)
            st = st_ref[...]
            oi_ref[rows, :] += lax.dot_general(qt_ref[rows, :], st.astype(BF16), _NT,
                                               preferred_element_type=F32)
            upd = lax.dot_general(iv_ref[rows, :], kst_ref[rows, :], _TN,
                                  preferred_element_type=F32)
            st_ref[...] = st * dec_ref[pl.ds(n * HGRN_BLOCK, 1), :] + upd
            return carry
        lax.fori_loop(0, n_sub, sub, 0)

        o = _rms(oi_ref[...], gn_ref[...])
        o_ref[...] = (o * jax.nn.sigmoid(g_ref[...])).astype(o_ref.dtype)

        @pl.when(t == n_t - 1)
        def _():
            sf_ref[...] = st_ref[...].T

    blk = pl.BlockSpec((tc, LANES), lambda b, h, t: (b * n_t + t, h))
    vec = pl.BlockSpec((1, LANES), lambda b, h, t: (0, h))
    st_spec = pl.BlockSpec((None, None, dk, dv), lambda b, h, t: (b, h, 0, 0))
    return pl.pallas_call(
        kernel,
        grid=(B, H, n_t),
        in_specs=[blk, blk, blk, blk, vec, vec, st_spec],
        out_specs=[blk, st_spec],
        out_shape=[jax.ShapeDtypeStruct((M, D), BF16), jax.ShapeDtypeStruct(s0.shape, F32)],
        scratch_shapes=[pltpu.VMEM((dv, dk), F32), pltpu.VMEM((tc, dk), BF16),
                        pltpu.VMEM((tc, dk), BF16), pltpu.VMEM((tc, dk), F32),
                        pltpu.VMEM((tc, dv), F32)],
        compiler_params=_params("arbitrary", "arbitrary", "arbitrary"),
        name="hgrn_scan",
    )(q, fl, iv, g, lb.reshape(1, D).astype(F32), gn.reshape(1, D).astype(F32), s0)


def _hgrn_layer(h, gamma, wts, lb, s0, *, B, T):
    w_parts, gn, wo = wts
    M, D = h.shape
    tm = _tile(T, 512)

    def cast_epilogue(j, acc, ctx):
        ctx["outs"][0][...] = acc.astype(ctx["outs"][0].dtype)

    parts = []
    for w, dt, nm in zip(w_parts, (F32, F32, BF16, F32), ("q", "f", "i", "g")):
        parts.append(_proj(h, w, tm=tm, gamma=gamma, outs=[_row_out(M, D, tm, dt)],
                           epilogue=cast_epilogue, name="hgrn_%s_proj" % nm)[0])
    o, s_new = _hgrn_scan(*parts, lb, gn, s0, B=B, T=T, tc=_tile(T, 256, HGRN_BLOCK))
    h = _out_proj(o, wo, h, T=T, name="hgrn_out_proj")
    return h, s_new


def _conv_ln_silu(u, buf, w_dw, b_dw, ln_g, ln_b, *, B, T, tc):
    M, D = u.shape
    W = w_dw.shape[0]
    halo = buf.shape[1]
    off = halo - (W - 1)
    n_t = T // tc
    assert T % tc == 0 and tc >= halo and off >= 0

    def kernel(u_ref, buf_ref, w_ref, bd_ref, lg_ref, lbias_ref, y_ref, ext_ref):
        @pl.when(pl.program_id(1) == 0)
        def _():
            ext_ref[0:halo, :] = buf_ref[...]
        ext_ref[halo:halo + tc, :] = u_ref[...]
        acc = jnp.zeros((tc, D), F32) + bd_ref[...]
        for j in range(W):
            acc = acc + w_ref[j:j + 1, :] * ext_ref[off + j:off + j + tc, :]
        mu = jnp.mean(acc, axis=-1, keepdims=True)
        d = acc - mu
        var = jnp.mean(d * d, axis=-1, keepdims=True)
        y = d * lax.rsqrt(var + EPS) * lg_ref[...] + lbias_ref[...]
        y_ref[...] = (y * jax.nn.sigmoid(y)).astype(y_ref.dtype)
        ext_ref[0:halo, :] = ext_ref[tc:tc + halo, :]

    vec = pl.BlockSpec((1, D), lambda b, t: (0, 0))
    return pl.pallas_call(
        kernel,
        grid=(B, n_t),
        in_specs=[pl.BlockSpec((tc, D), lambda b, t: (b * n_t + t, 0)),
                  pl.BlockSpec((None, halo, D), lambda b, t: (b, 0, 0)),
                  pl.BlockSpec((W, D), lambda b, t: (0, 0)), vec, vec, vec],
        out_specs=pl.BlockSpec((tc, D), lambda b, t: (b * n_t + t, 0)),
        out_shape=jax.ShapeDtypeStruct((M, D), BF16),
        scratch_shapes=[pltpu.VMEM((halo + tc, D), F32)],
        compiler_params=_params("arbitrary", "arbitrary"),
        name="conv_ln_silu",
    )(u, buf, w_dw.astype(F32), b_dw.reshape(1, D).astype(F32),
      ln_g.reshape(1, D).astype(F32), ln_b.reshape(1, D).astype(F32))


def _conv_layer(h, gamma, wts, buf, *, B, T):
    w_glu, b_glu, w_dw, b_dw, ln_g, ln_b, w_pw2, b_pw2 = wts
    M, D = h.shape
    W = w_dw.shape[0]
    assert T >= W - 1
    tm = _tile(T, 512)
    wt = w_glu.shape[1] // 2

    def glu_epilogue(j, acc, ctx):
        half = acc.shape[1] // 2
        ctx["outs"][0][...] = acc[:, :half] * jax.nn.sigmoid(acc[:, half:])

    u = _proj(h, w_glu, tm=tm, wt=wt, gamma=gamma, bias=b_glu,
              outs=[_row_out(M, D, tm, F32, wt=wt // 2)], epilogue=glu_epilogue,
              name="conv_glu_proj")[0]
    halo = -(-(W - 1) // SUBLANES) * SUBLANES
    buf_p = jnp.pad(buf.astype(F32), ((0, 0), (halo - (W - 1), 0), (0, 0)))
    y = _conv_ln_silu(u, buf_p, w_dw, b_dw, ln_g, ln_b, B=B, T=T, tc=_tile(T, 256))
    h = _out_proj(y, w_pw2, h, T=T, bias=b_pw2, name="conv_out_proj")
    return h, u.reshape(B, T, D)[:, T - (W - 1):, :]


def _glu_weights(w, b, n_tiles=2):
    K, N2 = w.shape
    tn = N2 // 2 // n_tiles
    wr = w.reshape(K, 2, n_tiles, tn).transpose(0, 2, 1, 3).reshape(K, N2)
    br = b.reshape(2, n_tiles, tn).transpose(1, 0, 2).reshape(N2)
    return wr, br


def kernel(x_prompt, x_sample, p_prompt, p_sample, cache_fox_k, cache_fox_v, cache_fox_logf, state_hgrn, cache_conv, norm_mix, norm_ffn, norm_ple, norm_final, w_in_a, b_f_a, w_o_a, w_in_b, lb_logits, gnorm_b, w_o_b, w_pw1, b_pw1, w_dw, b_dw, ln_g, ln_b, w_pw2, b_pw2, w_up, w_down, w_ple, w_pg):
    depth, D = norm_mix.shape
    n_mixers = 3
    lb_soft = jax.nn.softmax(lb_logits.astype(F32), axis=0)
    lower_bounds = jnp.cumsum(lb_soft, axis=0) - lb_soft[0]

    groups = []
    for x, p in ((x_prompt, p_prompt), (x_sample, p_sample)):
        B, T, _ = x.shape
        groups.append(dict(B=B, T=T, h=x.reshape(B * T, D), p=p.reshape(depth, B * T, -1)))
    is_sample = (False, True)

    fox_k, fox_v, fox_l = ([], []), ([], []), ([], [])
    hg_state, conv_buf = ([], []), ([], [])
    ia = ib = ic = 0
    for i in range(depth):
        kind = i % n_mixers
        if kind == 0:
            w = w_in_a[ia].astype(BF16)
            wts = (w[:, :D], w[:, D:2 * D], w[:, 2 * D:3 * D], w[:, 3 * D:], b_f_a[ia],
                   w_o_a[ia].astype(BF16))
        elif kind == 1:
            w = w_in_b[ib].astype(BF16)
            wts = (tuple(w[:, n * D:(n + 1) * D] for n in range(4)), gnorm_b[ib],
                   w_o_b[ib].astype(BF16))
        else:
            wts = _glu_weights(w_pw1[ic].astype(BF16), b_pw1[ic]) + (
                w_dw[ic], b_dw[ic], ln_g[ic], ln_b[ic], w_pw2[ic].astype(BF16), b_pw2[ic])
        wu, wd = w_up[i].astype(BF16), w_down[i].astype(BF16)
        wple, wpg = w_ple[i].astype(BF16), w_pg[i].astype(BF16)
        for gi, grp in enumerate(groups):
            B, T, h = grp["B"], grp["T"], grp["h"]
            if kind == 0:
                cache = ((cache_fox_k[ia], cache_fox_v[ia], cache_fox_logf[ia])
                         if is_sample[gi] else None)
                h, k_new, v_new, l_new = _fox_layer(h, norm_mix[i], wts, B=B, T=T, cache=cache)
                fox_k[gi].append(k_new)
                fox_v[gi].append(v_new)
                fox_l[gi].append(l_new)
            elif kind == 1:
                s0 = (state_hgrn[ib].astype(F32) if is_sample[gi]
                      else jnp.zeros((B,) + state_hgrn.shape[2:], F32))
                h, s_new = _hgrn_layer(h, norm_mix[i], wts, lower_bounds[i], s0, B=B, T=T)
                hg_state[gi].append(s_new)
            else:
                buf = (cache_conv[ic] if is_sample[gi]
                       else jnp.zeros((B, w_dw.shape[1] - 1, D), F32))
                h, b_new = _conv_layer(h, norm_mix[i], wts, buf, B=B, T=T)
                conv_buf[gi].append(b_new)
            h = _ffn(h, norm_ffn[i], wu, wd, T=T)
            h = _ple(h, grp["p"][i], norm_ple[i], wple, wpg, T=T,
                     final_gamma=norm_final if i == depth - 1 else None)
            grp["h"] = h
        if kind == 0:
            ia += 1
        elif kind == 1:
            ib += 1
        else:
            ic += 1

    ys = [grp["h"].reshape(grp["B"], grp["T"], D) for grp in groups]
    return (ys[0], ys[1],
            jnp.stack(fox_k[0]), jnp.stack(fox_v[0]), jnp.stack(fox_l[0]),
            jnp.stack(fox_k[1]), jnp.stack(fox_v[1]), jnp.stack(fox_l[1]),
            jnp.stack(hg_state[0]), jnp.stack(hg_state[1]),
            jnp.stack(conv_buf[0]), jnp.stack(conv_buf[1]))
```

```python
import functools

import jax
import jax.numpy as jnp
import numpy as np
from jax import lax
from jax.experimental import pallas as pl
from jax.experimental.pallas import tpu as pltpu

F32 = jnp.float32
BF16 = jnp.bfloat16
EPS = 1e-6
NEG_INF = -1e30
LOG2E = 1.4426950408889634
HGRN_BLOCK = 16
LANES = 128
SUBLANES = 8
VMEM_LIMIT_BYTES = 56 * 1024 * 1024

_NT = (((1,), (1,)), ((), ()))
_TN = (((0,), (0,)), ((), ()))


def _tile(n, pref, mult=SUBLANES):
    for d in range(min(n, pref), 0, -1):
        if n % d == 0 and d % mult == 0:
            return d
    return n


def _params(*sem):
    return pltpu.CompilerParams(dimension_semantics=sem, vmem_limit_bytes=VMEM_LIMIT_BYTES)


def _rms(x, g):
    return x * lax.rsqrt(jnp.mean(x * x, axis=-1, keepdims=True) + EPS) * g


def _split3(x):
    hi = x.astype(BF16)
    r1 = x - hi.astype(F32)
    mid = r1.astype(BF16)
    lo = (r1 - mid.astype(F32)).astype(BF16)
    return hi, mid, lo


def _dot_ones_rhs(x, mask_bf16):
    out = None
    for part in _split3(x):
        d = jnp.dot(part, mask_bf16, preferred_element_type=F32)
        out = d if out is None else out + d
    return out


def _proj(x, w, *, tm, wt=None, gamma=None, bias=None, head_major_x=False,
          extras=(), outs, epilogue, name):
    K, N = w.shape
    wt = N if wt is None else wt
    if head_major_x:
        B, H, T, dh = x.shape
        M = B * T
        n_t = T // tm
        x_spec = pl.BlockSpec((None, H, tm, dh), lambda i, j: (i // n_t, 0, i % n_t, 0))
    else:
        M = x.shape[0]
        x_spec = pl.BlockSpec((tm, K), lambda i, j: (i, 0))
    has_g = gamma is not None
    has_b = bias is not None
    assert M % tm == 0 and N % wt == 0
    assert has_g == (x.dtype == F32)

    def kernel(*refs):
        it = iter(refs)
        x_ref = next(it)
        g_ref = next(it) if has_g else None
        w_ref = next(it)
        b_ref = next(it) if has_b else None
        extra_refs = [next(it) for _ in extras]
        out_refs = [next(it) for _ in outs]
        xn_ref = next(it) if has_g else None
        j = pl.program_id(1)
        if has_g:
            @pl.when(j == 0)
            def _():
                xn_ref[...] = _rms(x_ref[...], g_ref[...]).astype(BF16)
            lhs = xn_ref[...]
        elif head_major_x:
            lhs = jnp.concatenate([x_ref[h] for h in range(x_ref.shape[0])], axis=1)
        else:
            lhs = x_ref[...]
        acc = jnp.dot(lhs, w_ref[...], preferred_element_type=F32)
        if has_b:
            acc = acc + b_ref[...]
        epilogue(j, acc, dict(x=x_ref, lhs=lhs, extras=extra_refs, outs=out_refs))

    in_arrays = [x]
    in_specs = [x_spec]
    if has_g:
        in_arrays.append(gamma.reshape(1, K).astype(F32))
        in_specs.append(pl.BlockSpec((1, K), lambda i, j: (0, 0)))
    in_arrays.append(w)
    in_specs.append(pl.BlockSpec((K, wt), lambda i, j: (0, j)))
    if has_b:
        in_arrays.append(bias.reshape(1, N).astype(F32))
        in_specs.append(pl.BlockSpec((1, wt), lambda i, j: (0, j)))
    for arr, blk, imap in extras:
        in_arrays.append(arr)
        in_specs.append(pl.BlockSpec(blk, imap))
    return pl.pallas_call(
        kernel,
        grid=(M // tm, N // wt),
        in_specs=in_specs,
        out_specs=[pl.BlockSpec(blk, imap) for _, blk, imap in outs],
        out_shape=[sds for sds, _, _ in outs],
        scratch_shapes=[pltpu.VMEM((tm, K), BF16)] if has_g else [],
        compiler_params=_params("arbitrary", "arbitrary"),
        name=name,
    )(*in_arrays)


def _store_head_major(ref, acc):
    for h in range(ref.shape[0]):
        ref[h] = acc[:, h * ref.shape[2]:(h + 1) * ref.shape[2]].astype(ref.dtype)


def _head_major_out(B, H, T, dh, tm):
    n_t = T // tm
    return (jax.ShapeDtypeStruct((B, H, T, dh), BF16), (None, H, tm, dh),
            lambda i, j: (i // n_t, 0, i % n_t, 0))


def _row_out(M, N, tm, dtype, wt=None):
    wt = N if wt is None else wt
    return (jax.ShapeDtypeStruct((M, N), dtype), (tm, wt), lambda i, j: (i, j))


def _log_sigmoid(x):
    return jnp.minimum(x, 0.0) - jnp.log(1.0 + jnp.exp(-jnp.abs(x)))


def _fox_project(h, gamma, wq, wk, wv, wf, bf, *, B, T, H):
    M, D = h.shape
    dh = D // H
    tm = _tile(T, 512)
    scale = dh ** -0.5 * LOG2E

    def q_epilogue(j, acc, ctx):
        q_ref, logf_ref = ctx["outs"]
        wf_ref, bf_ref = ctx["extras"]
        _store_head_major(q_ref, acc * scale)
        fl = jnp.dot(ctx["lhs"], wf_ref[...], preferred_element_type=F32)
        logf_ref[...] = _log_sigmoid(fl[:, :H] + bf_ref[...])

    wf_pad = jnp.pad(wf, ((0, 0), (0, LANES - H)))
    q_hm, logf = _proj(
        h, wq, tm=tm, gamma=gamma,
        extras=[(wf_pad, (D, LANES), lambda i, j: (0, 0)),
                (bf.reshape(1, H).astype(F32), (1, H), lambda i, j: (0, 0))],
        outs=[_head_major_out(B, H, T, dh, tm), _row_out(M, H, tm, F32)],
        epilogue=q_epilogue, name="fox_q_proj")

    def kv_epilogue(j, acc, ctx):
        full_ref, hm_ref = ctx["outs"]
        full_ref[...] = acc
        _store_head_major(hm_ref, acc)

    kv = []
    for w, nm in ((wk, "fox_k_proj"), (wv, "fox_v_proj")):
        kv.append(_proj(h, w, tm=tm, gamma=gamma,
                        outs=[_row_out(M, D, tm, F32), _head_major_out(B, H, T, dh, tm)],
                        epilogue=kv_epilogue, name=nm))
    (k32, k_hm), (v32, v_hm) = kv
    return q_hm, k32, k_hm, v32, v_hm, logf


def _cumsum_lanes(x):
    R, L = x.shape
    assert L % LANES == 0

    def kernel(x_ref, o_ref):
        upper = (lax.broadcasted_iota(jnp.int32, (LANES, LANES), 0)
                 <= lax.broadcasted_iota(jnp.int32, (LANES, LANES), 1)).astype(BF16)
        carry = jnp.zeros((R, 1), F32)
        for c in range(L // LANES):
            sl = slice(c * LANES, (c + 1) * LANES)
            loc = _dot_ones_rhs(x_ref[:, sl], upper)
            o_ref[:, sl] = loc + carry
            carry = carry + loc[:, LANES - 1:LANES]

    return pl.pallas_call(
        kernel,
        out_shape=jax.ShapeDtypeStruct((R, L), F32),
        compiler_params=pltpu.CompilerParams(vmem_limit_bytes=VMEM_LIMIT_BYTES),
        name="fox_cumsum",
    )(x)


def _fox_attention(q, k, v, cq, ck, *, q_off, tq, tk):
    B, H, Tq, dh = q.shape
    Tk = k.shape[2]
    nq, nk = Tq // tq, Tk // tk
    assert Tq % tq == 0 and Tk % tk == 0 and tk % LANES == 0 and dh == LANES
    n_c = tk // LANES
    unroll = next(u for u in (8, 4, 2, 1) if H % u == 0)

    def last_k(qi):
        return jnp.minimum(nk - 1, (q_off + (qi + 1) * tq - 1) // tk)

    def kernel(q_ref, k_ref, v_ref, cq_ref, ck_ref, o_ref, m_ref, acc_ref, cqs_ref):
        qi = pl.program_id(1)
        ki = pl.program_id(2)

        @pl.when(ki == 0)
        def _():
            m_ref[...] = jnp.full(m_ref.shape, NEG_INF, F32)
            acc_ref[...] = jnp.zeros(acc_ref.shape, F32)
            cqv = cq_ref[...] * LOG2E
            for h in range(H):
                cqs_ref[h] = jnp.broadcast_to(cqv[:, h:h + 1], (tq, LANES))

        q_lo = q_off + qi * tq
        k_lo = ki * tk
        active = k_lo <= q_lo + tq - 1
        crosses = k_lo + tk - 1 > q_lo

        def one_head(h, masked):
            s = lax.dot_general(q_ref[h], k_ref[h], _NT, preferred_element_type=F32)
            cqh = cqs_ref[h]
            ckh = ck_ref[pl.ds(h, 1), :] * LOG2E
            if masked:
                rel = (lax.broadcasted_iota(jnp.int32, (tq, LANES), 1)
                       - lax.broadcasted_iota(jnp.int32, (tq, LANES), 0))
            chunks = []
            for c in range(n_c):
                sl = slice(c * LANES, (c + 1) * LANES)
                sc = s[:, sl] - ckh[:, sl]
                if masked:
                    sc = jnp.where(rel <= q_lo - k_lo - c * LANES, sc, NEG_INF)
                chunks.append(sc)
            mx = functools.reduce(jnp.maximum, chunks)
            m_prev = m_ref[h]
            m_new = jnp.maximum(m_prev, jnp.max(mx, axis=1, keepdims=True) + cqh)
            alpha = jnp.exp2(m_prev - m_new)
            shift = m_new - cqh
            p = jnp.concatenate([jnp.exp2(sc - shift).astype(BF16) for sc in chunks], axis=1)
            v_ones = jnp.concatenate([v_ref[h], jnp.ones((tk, dh), BF16)], axis=1)
            pv = jnp.dot(p, v_ones, preferred_element_type=F32)
            acc_ref[h] = acc_ref[h] * jnp.concatenate([alpha, alpha], axis=1) + pv
            m_ref[h] = m_new

        def heads(masked):
            def body(i, carry):
                for u in range(unroll):
                    one_head(i * unroll + u, masked)
                return carry
            lax.fori_loop(0, H // unroll, body, 0)

        @pl.when(jnp.logical_and(active, crosses))
        def _():
            heads(True)

        @pl.when(jnp.logical_and(active, jnp.logical_not(crosses)))
        def _():
            heads(False)

        @pl.when(ki == nk - 1)
        def _():
            a = acc_ref[...]
            o_ref[...] = (a[:, :, :dh] / a[:, :, dh:]).astype(o_ref.dtype)

    kv_spec = pl.BlockSpec((None, H, tk, dh),
                           lambda b, qi, ki: (b, 0, jnp.minimum(ki, last_k(qi)), 0))
    return pl.pallas_call(
        kernel,
        grid=(B, nq, nk),
        in_specs=[
            pl.BlockSpec((None, H, tq, dh), lambda b, qi, ki: (b, 0, qi, 0)),
            kv_spec, kv_spec,
            pl.BlockSpec((None, tq, H), lambda b, qi, ki: (b, qi, 0)),
            pl.BlockSpec((None, H, tk), lambda b, qi, ki: (b, 0, jnp.minimum(ki, last_k(qi)))),
        ],
        out_specs=pl.BlockSpec((None, H, tq, dh), lambda b, qi, ki: (b, 0, qi, 0)),
        out_shape=jax.ShapeDtypeStruct((B, H, Tq, dh), BF16),
        scratch_shapes=[pltpu.VMEM((H, tq, LANES), F32), pltpu.VMEM((H, tq, 2 * dh), F32),
                        pltpu.VMEM((H, tq, LANES), F32)],
        compiler_params=_params("arbitrary", "arbitrary", "arbitrary"),
        name="fox_attention",
    )(q, k, v, cq, ck)


def _to_head_major(x):
    return jnp.transpose(x, (0, 2, 1, 3)).astype(BF16)


def _fox_layer(h, gamma, wts, *, B, T, cache=None):
    wq, wk, wv, wf, bf, wo = wts
    M, D = h.shape
    H = bf.shape[0]
    dh = D // H
    q_hm, k32, k_hm, v32, v_hm, logf = _fox_project(h, gamma, wq, wk, wv, wf, bf, B=B, T=T, H=H)
    logf_bth = logf.reshape(B, T, H)
    if cache is None:
        P = 0
        logf_all = logf_bth
    else:
        ck_, cv_, cl_ = cache
        P = ck_.shape[1]
        k_hm = jnp.concatenate([_to_head_major(ck_), k_hm], axis=2)
        v_hm = jnp.concatenate([_to_head_major(cv_), v_hm], axis=2)
        logf_all = jnp.concatenate([cl_.astype(F32), logf_bth], axis=1)
    Tk = P + T
    tk = _tile(Tk, 512, LANES) if Tk % LANES == 0 else -(-Tk // LANES) * LANES
    pad = -Tk % tk
    if pad:
        k_hm = jnp.pad(k_hm, ((0, 0), (0, 0), (0, pad), (0, 0)))
        v_hm = jnp.pad(v_hm, ((0, 0), (0, 0), (0, pad), (0, 0)))
        logf_all = jnp.pad(logf_all, ((0, 0), (0, pad), (0, 0)))
    rows = jnp.transpose(logf_all, (0, 2, 1)).reshape(B * H, Tk + pad)
    ck = _cumsum_lanes(rows).reshape(B, H, Tk + pad)
    cq = jnp.transpose(ck[:, :, P:P + T], (0, 2, 1))
    o_hm = _fox_attention(q_hm, k_hm, v_hm, cq, ck, q_off=P, tq=_tile(T, 512), tk=tk)
    h = _out_proj(o_hm, wo, h, T=T, head_major=True, name="fox_out_proj")
    return h, k32.reshape(B, T, H, dh), v32.reshape(B, T, H, dh), logf_bth


def _out_proj(x, w, h, *, T, bias=None, head_major=False, name):
    M, D = h.shape
    tm = _tile(T if head_major else M, 512, 2 * SUBLANES)

    def epilogue(j, acc, ctx):
        ctx["outs"][0][...] = ctx["extras"][0][...] + acc

    return _proj(x, w, tm=tm, bias=bias, head_major_x=head_major,
                 extras=[(h, (tm, D), lambda i, j: (i, 0))],
                 outs=[_row_out(M, D, tm, F32)], epilogue=epilogue, name=name)[0]


def _ffn(h, gamma, w_up, w_down):
    M, D = h.shape
    F = w_up.shape[1]
    tm = _tile(M, 512)
    tf = _tile(F, 1024, LANES)

    def kernel(x_ref, g_ref, wu_ref, wd_ref, o_ref, xn_ref):
        @pl.when(pl.program_id(1) == 0)
        def _():
            x = x_ref[...]
            xn_ref[...] = _rms(x, g_ref[...]).astype(BF16)
            o_ref[...] = x
        hid = jnp.dot(xn_ref[...], wu_ref[...], preferred_element_type=F32)
        hid = jnp.square(jnp.maximum(hid, 0.0)).astype(BF16)
        o_ref[...] += jnp.dot(hid, wd_ref[...], preferred_element_type=F32)

    return pl.pallas_call(
        kernel,
        grid=(M // tm, F // tf),
        in_specs=[pl.BlockSpec((tm, D), lambda i, f: (i, 0)),
                  pl.BlockSpec((1, D), lambda i, f: (0, 0)),
                  pl.BlockSpec((D, tf), lambda i, f: (0, f)),
                  pl.BlockSpec((tf, D), lambda i, f: (f, 0))],
        out_specs=pl.BlockSpec((tm, D), lambda i, f: (i, 0)),
        out_shape=jax.ShapeDtypeStruct((M, D), F32),
        scratch_shapes=[pltpu.VMEM((tm, D), BF16)],
        compiler_params=_params("arbitrary", "arbitrary"),
        name="sq_relu_mlp",
    )(h, gamma.reshape(1, D).astype(F32), w_up, w_down)


def _ple(h, p, gamma, w_ple, w_pg, *, final_gamma=None):
    M, D = h.shape
    E = p.shape[1]
    tm = _tile(M, 256)
    fin = final_gamma is not None

    def epilogue(j, acc, ctx):
        p_ref, wple_ref = ctx["extras"][:2]
        e = jnp.dot(p_ref[...].astype(BF16), wple_ref[...], preferred_element_type=F32)
        out = ctx["x"][...] + e * jax.nn.sigmoid(acc)
        if fin:
            out = _rms(out, ctx["extras"][2][...])
        ctx["outs"][0][...] = out

    extras = [(p, (tm, E), lambda i, j: (i, 0)), (w_ple, (E, D), lambda i, j: (0, 0))]
    if fin:
        extras.append((final_gamma.reshape(1, D).astype(F32), (1, D), lambda i, j: (0, 0)))
    return _proj(h, w_pg, tm=tm, gamma=gamma, extras=extras,
                 outs=[_row_out(M, D, tm, F32)], epilogue=epilogue, name="ple_add")[0]


def _hgrn_level_codes(C):
    t = np.arange(C)[:, None]
    s = np.arange(C)[None, :]
    code = np.where((t // HGRN_BLOCK == s // HGRN_BLOCK) & (s <= t), 1, 0)
    level, R = 2, HGRN_BLOCK
    while R < C:
        code = np.where(((t // R) % 2 == 1) & (s // R == t // R - 1), level, code)
        level, R = level + 1, 2 * R
    return jnp.asarray(code, jnp.int32)


def _hgrn_scan(q, fl, iv, g, lb, gn, s0, *, B, T, tc, C):
    M, D = q.shape
    H, dk, dv = s0.shape[1:]
    assert dk == LANES and dv == LANES and T % tc == 0 and tc % C == 0
    assert C % HGRN_BLOCK == 0 and (C // HGRN_BLOCK) & (C // HGRN_BLOCK - 1) == 0
    n_t = T // tc
    steps = [1 << i for i in range(HGRN_BLOCK.bit_length() - 1)]

    def kernel(q_ref, fl_ref, iv_ref, g_ref, lb_ref, gn_ref, s0_ref, code_ref, o_ref, sf_ref,
               st_ref):
        t = pl.program_id(2)

        @pl.when(t == 0)
        def _():
            st_ref[...] = s0_ref[...].T

        lbv = lb_ref[...]
        gnv = gn_ref[...]
        code = code_ref[...]
        row = lax.broadcasted_iota(jnp.int32, (C, LANES), 0)
        pos = jnp.bitwise_and(row, HGRN_BLOCK - 1)
        st = st_ref[...]
        for c in range(tc // C):
            rs = slice(c * C, (c + 1) * C)
            qv = q_ref[rs, :]
            vv = iv_ref[rs, :]
            f = lbv + (1.0 - lbv) * jax.nn.sigmoid(fl_ref[rs, :])
            kk = 1.0 - f
            L = jnp.log(f)
            for sh in steps:
                L = L + jnp.where(pos >= sh, pltpu.roll(L, sh, 0), 0.0)
            Tt = jnp.where(pos == HGRN_BLOCK - 1, L, 0.0)
            for sh in steps:
                Tt = Tt + pltpu.roll(Tt, C - sh, 0)
            a = lax.dot_general((qv * jnp.exp(L)).astype(BF16), (kk * jnp.exp(-L)).astype(BF16),
                                _NT, preferred_element_type=F32)
            A = jnp.where(code == 1, a, 0.0)
            level, R = 2, HGRN_BLOCK
            while R < C:
                upper = jnp.bitwise_and(row, R) != 0
                x = (jnp.where(upper, qv, kk)
                     * jnp.exp(jnp.where(upper, L, Tt - L))).astype(BF16)
                a = lax.dot_general(x, x, _NT, preferred_element_type=F32)
                A = jnp.where(code == level, a, A)
                below = pltpu.roll(Tt, R, 0)
                above = pltpu.roll(Tt, C - R, 0)
                L = L + jnp.where(upper, below, 0.0)
                Tt = Tt + jnp.where(upper, below, above)
                level, R = level + 1, 2 * R
            o = jnp.dot(A.astype(BF16), vv, preferred_element_type=F32)
            o = o + lax.dot_general((qv * jnp.exp(L)).astype(BF16), st.astype(BF16), _NT,
                                    preferred_element_type=F32)
            k_end = (kk * jnp.exp(Tt - L)).astype(BF16)
            st = st * jnp.exp(Tt[0:1, :]) + lax.dot_general(vv, k_end, _TN,
                                                            preferred_element_type=F32)
            o = _rms(o, gnv) * jax.nn.sigmoid(g_ref[rs, :])
            o_ref[rs, :] = o.astype(o_ref.dtype)
        st_ref[...] = st

        @pl.when(t == n_t - 1)
        def _():
            sf_ref[...] = st.T

    blk = pl.BlockSpec((tc, LANES), lambda b, h, t: (b * n_t + t, h))
    vec = pl.BlockSpec((1, LANES), lambda b, h, t: (0, h))
    st_spec = pl.BlockSpec((None, None, dk, dv), lambda b, h, t: (b, h, 0, 0))
    return pl.pallas_call(
        kernel,
        grid=(B, H, n_t),
        in_specs=[blk, blk, blk, blk, vec, vec, st_spec,
                  pl.BlockSpec((C, C), lambda b, h, t: (0, 0))],
        out_specs=[blk, st_spec],
        out_shape=[jax.ShapeDtypeStruct((M, D), BF16), jax.ShapeDtypeStruct(s0.shape, F32)],
        scratch_shapes=[pltpu.VMEM((dv, dk), F32)],
        compiler_params=_params("arbitrary", "arbitrary", "arbitrary"),
        name="hgrn_scan",
    )(q, fl, iv, g, lb.reshape(1, D).astype(F32), gn.reshape(1, D).astype(F32), s0,
      _hgrn_level_codes(C))


def _hgrn_layer(h, gamma, wts, lb, s0, *, B, T):
    w_parts, gn, wo = wts
    M, D = h.shape
    tm = _tile(M, 512, 2 * SUBLANES)

    def cast_epilogue(j, acc, ctx):
        ctx["outs"][0][...] = acc.astype(ctx["outs"][0].dtype)

    parts = []
    for w, dt, nm in zip(w_parts, (F32, F32, BF16, F32), ("q", "f", "i", "g")):
        parts.append(_proj(h, w, tm=tm, gamma=gamma, outs=[_row_out(M, D, tm, dt)],
                           epilogue=cast_epilogue, name="hgrn_%s_proj" % nm)[0])
    C = LANES
    while T % C:
        C //= 2
    o, s_new = _hgrn_scan(*parts, lb, gn, s0, B=B, T=T, tc=_tile(T, 1024, C), C=C)
    h = _out_proj(o, wo, h, T=T, name="hgrn_out_proj")
    return h, s_new


def _conv_ln_silu(u, buf, w_dw, b_dw, ln_g, ln_b, *, B, T, tc):
    M, D = u.shape
    W = w_dw.shape[0]
    halo = buf.shape[1]
    off = halo - (W - 1)
    n_t = T // tc
    assert T % tc == 0 and tc >= halo and off >= 0
    cw = _tile(D, 4 * LANES, LANES)
    n_groups = -(-W // SUBLANES)

    def kernel(u_ref, buf_ref, w_ref, bd_ref, lg_ref, lbias_ref, y_ref, ext_ref, acc_ref,
               sh_ref):
        @pl.when(pl.program_id(1) == 0)
        def _():
            ext_ref[0:halo, :] = buf_ref[...]
        ext_ref[halo:halo + tc, :] = u_ref[...]
        for cb in range(D // cw):
            cols = slice(cb * cw, (cb + 1) * cw)
            part = jnp.zeros((tc, cw), F32) + bd_ref[:, cols]
            for r in range(min(SUBLANES, W)):
                taps = range(r, W, SUBLANES)
                span = tc + (len(taps) - 1) * SUBLANES
                sh_ref[0:span, :] = ext_ref[off + r:off + r + span, cols]
                for m, j in enumerate(taps):
                    part = part + (w_ref[j:j + 1, cols]
                                   * sh_ref[m * SUBLANES:m * SUBLANES + tc, :])
            acc_ref[:, cols] = part
        acc = acc_ref[...]
        mu = jnp.mean(acc, axis=-1, keepdims=True)
        d = acc - mu
        var = jnp.mean(d * d, axis=-1, keepdims=True)
        y = d * lax.rsqrt(var + EPS) * lg_ref[...] + lbias_ref[...]
        y_ref[...] = (y * jax.nn.sigmoid(y)).astype(y_ref.dtype)
        ext_ref[0:halo, :] = ext_ref[tc:tc + halo, :]

    vec = pl.BlockSpec((1, D), lambda b, t: (0, 0))
    return pl.pallas_call(
        kernel,
        grid=(B, n_t),
        in_specs=[pl.BlockSpec((tc, D), lambda b, t: (b * n_t + t, 0)),
                  pl.BlockSpec((None, halo, D), lambda b, t: (b, 0, 0)),
                  pl.BlockSpec((W, D), lambda b, t: (0, 0)), vec, vec, vec],
        out_specs=pl.BlockSpec((tc, D), lambda b, t: (b * n_t + t, 0)),
        out_shape=jax.ShapeDtypeStruct((M, D), BF16),
        scratch_shapes=[pltpu.VMEM((halo + tc, D), F32), pltpu.VMEM((tc, D), F32),
                        pltpu.VMEM((tc + (n_groups - 1) * SUBLANES, cw), F32)],
        compiler_params=_params("arbitrary", "arbitrary"),
        name="conv_ln_silu",
    )(u, buf, w_dw.astype(F32), b_dw.reshape(1, D).astype(F32),
      ln_g.reshape(1, D).astype(F32), ln_b.reshape(1, D).astype(F32))


def _conv_layer(h, gamma, wts, buf, *, B, T):
    w_glu, b_glu, w_dw, b_dw, ln_g, ln_b, w_pw2, b_pw2 = wts
    M, D = h.shape
    W = w_dw.shape[0]
    assert T >= W - 1
    tm = _tile(M, 512)
    wt = w_glu.shape[1] // 2

    def glu_epilogue(j, acc, ctx):
        half = acc.shape[1] // 2
        ctx["outs"][0][...] = acc[:, :half] * jax.nn.sigmoid(acc[:, half:])

    u = _proj(h, w_glu, tm=tm, wt=wt, gamma=gamma, bias=b_glu,
              outs=[_row_out(M, D, tm, F32, wt=wt // 2)], epilogue=glu_epilogue,
              name="conv_glu_proj")[0]
    halo = -(-(W - 1) // SUBLANES) * SUBLANES
    buf_p = jnp.pad(buf.astype(F32), ((0, 0), (halo - (W - 1), 0), (0, 0)))
    y = _conv_ln_silu(u, buf_p, w_dw, b_dw, ln_g, ln_b, B=B, T=T, tc=_tile(T, 256))
    h = _out_proj(y, w_pw2, h, T=T, bias=b_pw2, name="conv_out_proj")
    return h, u.reshape(B, T, D)[:, T - (W - 1):, :]


def _glu_weights(w, b, n_tiles=2):
    K, N2 = w.shape
    tn = N2 // 2 // n_tiles
    wr = w.reshape(K, 2, n_tiles, tn).transpose(0, 2, 1, 3).reshape(K, N2)
    br = b.reshape(2, n_tiles, tn).transpose(1, 0, 2).reshape(N2)
    return wr, br


def kernel(x_prompt, x_sample, p_prompt, p_sample, cache_fox_k, cache_fox_v, cache_fox_logf, state_hgrn, cache_conv, norm_mix, norm_ffn, norm_ple, norm_final, w_in_a, b_f_a, w_o_a, w_in_b, lb_logits, gnorm_b, w_o_b, w_pw1, b_pw1, w_dw, b_dw, ln_g, ln_b, w_pw2, b_pw2, w_up, w_down, w_ple, w_pg):
    depth, D = norm_mix.shape
    n_mixers = 3
    lb_soft = jax.nn.softmax(lb_logits.astype(F32), axis=0)
    lower_bounds = jnp.cumsum(lb_soft, axis=0) - lb_soft[0]

    groups = []
    for x, p in ((x_prompt, p_prompt), (x_sample, p_sample)):
        B, T, _ = x.shape
        groups.append(dict(B=B, T=T, h=x.reshape(B * T, D), p=p.reshape(depth, B * T, -1)))
    is_sample = (False, True)

    fox_k, fox_v, fox_l = ([], []), ([], []), ([], [])
    hg_state, conv_buf = ([], []), ([], [])
    ia = ib = ic = 0
    for i in range(depth):
        kind = i % n_mixers
        if kind == 0:
            w = w_in_a[ia].astype(BF16)
            wts = (w[:, :D], w[:, D:2 * D], w[:, 2 * D:3 * D], w[:, 3 * D:], b_f_a[ia],
                   w_o_a[ia].astype(BF16))
        elif kind == 1:
            w = w_in_b[ib].astype(BF16)
            wts = (tuple(w[:, n * D:(n + 1) * D] for n in range(4)), gnorm_b[ib],
                   w_o_b[ib].astype(BF16))
        else:
            wts = _glu_weights(w_pw1[ic].astype(BF16), b_pw1[ic]) + (
                w_dw[ic], b_dw[ic], ln_g[ic], ln_b[ic], w_pw2[ic].astype(BF16), b_pw2[ic])
        wu, wd = w_up[i].astype(BF16), w_down[i].astype(BF16)
        wple, wpg = w_ple[i].astype(BF16), w_pg[i].astype(BF16)
        for gi, grp in enumerate(groups):
            B, T, h = grp["B"], grp["T"], grp["h"]
            if kind == 0:
                cache = ((cache_fox_k[ia], cache_fox_v[ia], cache_fox_logf[ia])
                         if is_sample[gi] else None)
                h, k_new, v_new, l_new = _fox_layer(h, norm_mix[i], wts, B=B, T=T, cache=cache)
                fox_k[gi].append(k_new)
                fox_v[gi].append(v_new)
                fox_l[gi].append(l_new)
            elif kind == 1:
                s0 = (state_hgrn[ib].astype(F32) if is_sample[gi]
                      else jnp.zeros((B,) + state_hgrn.shape[2:], F32))
                h, s_new = _hgrn_layer(h, norm_mix[i], wts, lower_bounds[i], s0, B=B, T=T)
                hg_state[gi].append(s_new)
            else:
                buf = (cache_conv[ic] if is_sample[gi]
                       else jnp.zeros((B, w_dw.shape[1] - 1, D), F32))
                h, b_new = _conv_layer(h, norm_mix[i], wts, buf, B=B, T=T)
                conv_buf[gi].append(b_new)
            h = _ffn(h, norm_ffn[i], wu, wd)
            h = _ple(h, grp["p"][i], norm_ple[i], wple, wpg,
                     final_gamma=norm_final if i == depth - 1 else None)
            grp["h"] = h
        if kind == 0:
            ia += 1
        elif kind == 1:
            ib += 1
        else:
            ic += 1

    ys = [grp["h"].reshape(grp["B"], grp["T"], D) for grp in groups]
    return (ys[0], ys[1],
            jnp.stack(fox_k[0]), jnp.stack(fox_v[0]), jnp.stack(fox_l[0]),
            jnp.stack(fox_k[1]), jnp.stack(fox_v[1]), jnp.stack(fox_l[1]),
            jnp.stack(hg_state[0]), jnp.stack(hg_state[1]),
            jnp.stack(conv_buf[0]), jnp.stack(conv_buf[1]))
```

```python
import functools

import jax
import jax.numpy as jnp
import numpy as np
from jax import lax
from jax.experimental import pallas as pl
from jax.experimental.pallas import tpu as pltpu

F32 = jnp.float32
BF16 = jnp.bfloat16
EPS = 1e-6
NEG_INF = -1e30
LOG2E = 1.4426950408889634
HGRN_BLOCK = 16
LANES = 128
SUBLANES = 8
VMEM_LIMIT_BYTES = 56 * 1024 * 1024

_NT = (((1,), (1,)), ((), ()))
_TN = (((0,), (0,)), ((), ()))


def _tile(n, pref, mult=SUBLANES):
    for d in range(min(n, pref), 0, -1):
        if n % d == 0 and d % mult == 0:
            return d
    return n


def _params(*sem):
    return pltpu.CompilerParams(dimension_semantics=sem, vmem_limit_bytes=VMEM_LIMIT_BYTES)


def _rms(x, g):
    return x * lax.rsqrt(jnp.mean(x * x, axis=-1, keepdims=True) + EPS) * g


def _split3(x):
    hi = x.astype(BF16)
    r1 = x - hi.astype(F32)
    mid = r1.astype(BF16)
    lo = (r1 - mid.astype(F32)).astype(BF16)
    return hi, mid, lo


def _dot_ones_rhs(x, mask_bf16):
    out = None
    for part in _split3(x):
        d = jnp.dot(part, mask_bf16, preferred_element_type=F32)
        out = d if out is None else out + d
    return out


def _proj(x, w, *, tm, wt=None, gamma=None, bias=None, head_major_x=False,
          extras=(), outs, epilogue, name):
    K, N = w.shape
    wt = N if wt is None else wt
    if head_major_x:
        B, H, T, dh = x.shape
        M = B * T
        n_t = T // tm
        x_spec = pl.BlockSpec((None, H, tm, dh), lambda i, j: (i // n_t, 0, i % n_t, 0))
    else:
        M = x.shape[0]
        x_spec = pl.BlockSpec((tm, K), lambda i, j: (i, 0))
    has_g = gamma is not None
    has_b = bias is not None
    assert M % tm == 0 and N % wt == 0
    assert has_g == (x.dtype == F32)

    def kernel(*refs):
        it = iter(refs)
        x_ref = next(it)
        g_ref = next(it) if has_g else None
        w_ref = next(it)
        b_ref = next(it) if has_b else None
        extra_refs = [next(it) for _ in extras]
        out_refs = [next(it) for _ in outs]
        xn_ref = next(it) if has_g else None
        j = pl.program_id(1)
        if has_g:
            @pl.when(j == 0)
            def _():
                xn_ref[...] = _rms(x_ref[...], g_ref[...]).astype(BF16)
            lhs = xn_ref[...]
        elif head_major_x:
            lhs = jnp.concatenate([x_ref[h] for h in range(x_ref.shape[0])], axis=1)
        else:
            lhs = x_ref[...]
        acc = jnp.dot(lhs, w_ref[...], preferred_element_type=F32)
        if has_b:
            acc = acc + b_ref[...]
        epilogue(j, acc, dict(x=x_ref, lhs=lhs, extras=extra_refs, outs=out_refs))

    in_arrays = [x]
    in_specs = [x_spec]
    if has_g:
        in_arrays.append(gamma.reshape(1, K).astype(F32))
        in_specs.append(pl.BlockSpec((1, K), lambda i, j: (0, 0)))
    in_arrays.append(w)
    in_specs.append(pl.BlockSpec((K, wt), lambda i, j: (0, j)))
    if has_b:
        in_arrays.append(bias.reshape(1, N).astype(F32))
        in_specs.append(pl.BlockSpec((1, wt), lambda i, j: (0, j)))
    for arr, blk, imap in extras:
        in_arrays.append(arr)
        in_specs.append(pl.BlockSpec(blk, imap))
    return pl.pallas_call(
        kernel,
        grid=(M // tm, N // wt),
        in_specs=in_specs,
        out_specs=[pl.BlockSpec(blk, imap) for _, blk, imap in outs],
        out_shape=[sds for sds, _, _ in outs],
        scratch_shapes=[pltpu.VMEM((tm, K), BF16)] if has_g else [],
        compiler_params=_params("arbitrary", "arbitrary"),
        name=name,
    )(*in_arrays)


def _store_head_major(ref, acc):
    for h in range(ref.shape[0]):
        ref[h] = acc[:, h * ref.shape[2]:(h + 1) * ref.shape[2]].astype(ref.dtype)


def _head_major_out(B, H, T, dh, tm):
    n_t = T // tm
    return (jax.ShapeDtypeStruct((B, H, T, dh), BF16), (None, H, tm, dh),
            lambda i, j: (i // n_t, 0, i % n_t, 0))


def _row_out(M, N, tm, dtype, wt=None):
    wt = N if wt is None else wt
    return (jax.ShapeDtypeStruct((M, N), dtype), (tm, wt), lambda i, j: (i, j))


def _log_sigmoid(x):
    return jnp.minimum(x, 0.0) - jnp.log(1.0 + jnp.exp(-jnp.abs(x)))


def _fox_project(h, gamma, wq, wk, wv, wf, bf, earlier, stack, *, B, T, H):
    M, D = h.shape
    dh = D // H
    tm = _tile(T, 512)
    scale = dh ** -0.5 * LOG2E

    def q_epilogue(j, acc, ctx):
        q_ref, logf_ref = ctx["outs"]
        wf_ref, bf_ref = ctx["extras"]
        _store_head_major(q_ref, acc * scale)
        fl = jnp.dot(ctx["lhs"], wf_ref[...], preferred_element_type=F32)
        logf_ref[...] = _log_sigmoid(fl[:, :H] + bf_ref[...])

    wf_pad = jnp.pad(wf, ((0, 0), (0, LANES - H)))
    q_hm, logf = _proj(
        h, wq, tm=tm, gamma=gamma,
        extras=[(wf_pad, (D, LANES), lambda i, j: (0, 0)),
                (bf.reshape(1, H).astype(F32), (1, H), lambda i, j: (0, 0))],
        outs=[_head_major_out(B, H, T, dh, tm), _row_out(M, H, tm, F32)],
        epilogue=q_epilogue, name="fox_q_proj")

    k32, k_hm = _fox_kv_proj(h, gamma, wk, earlier[0], stack, B=B, T=T, H=H, tm=tm,
                             name="fox_k_proj")
    v32, v_hm = _fox_kv_proj(h, gamma, wv, earlier[1], stack, B=B, T=T, H=H, tm=tm,
                             name="fox_v_proj")
    return q_hm, k32, k_hm, v32, v_hm, logf


def _fox_kv_proj(h, gamma, w, earlier, stack, *, B, T, H, tm, name):
    M, D = h.shape
    dh = D // H
    n_t = T // tm
    n = M // tm
    n_e = len(earlier)
    assert stack or not n_e

    def kernel(x_ref, g_ref, w_ref, *rest):
        earlier_refs = rest[:n_e]
        full_ref, hm_ref, stage_ref, sem_ref, esem_ref = rest[n_e:]
        own = full_ref.at[n_e] if stack else full_ref
        i = pl.program_id(0)
        slot = i % 2

        def head_copy(step, slot, h):
            t0 = pl.multiple_of((step % n_t) * tm, tm)
            return pltpu.make_async_copy(stage_ref.at[slot, :, pl.ds(h * dh, dh)],
                                         own.at[step // n_t, pl.ds(t0, tm), h, :],
                                         sem_ref.at[slot])

        def earlier_copy(e):
            return pltpu.make_async_copy(earlier_refs[e], full_ref.at[e], esem_ref.at[e])

        @pl.when(i == 0)
        def _():
            for e in range(n_e):
                earlier_copy(e).start()

        @pl.when(i >= 2)
        def _():
            for hd in range(H):
                head_copy(i - 2, slot, hd).wait()

        acc = jnp.dot(_rms(x_ref[...], g_ref[...]).astype(BF16), w_ref[...],
                      preferred_element_type=F32)
        stage_ref[slot] = acc
        _store_head_major(hm_ref, acc)
        for hd in range(H):
            head_copy(i, slot, hd).start()

        @pl.when(i == n - 1)
        def _():
            if n >= 2:
                for hd in range(H):
                    head_copy(i - 1, 1 - slot, hd).wait()
            for hd in range(H):
                head_copy(i, slot, hd).wait()
            for e in range(n_e):
                earlier_copy(e).wait()

    shape = (n_e + 1, B, T, H, dh) if stack else (B, T, H, dh)
    return pl.pallas_call(
        kernel,
        grid=(n,),
        in_specs=[pl.BlockSpec((tm, D), lambda i: (i, 0)),
                  pl.BlockSpec((1, D), lambda i: (0, 0)),
                  pl.BlockSpec((D, D), lambda i: (0, 0))]
        + [pl.BlockSpec(memory_space=pl.ANY)] * n_e,
        out_specs=[pl.BlockSpec(memory_space=pl.ANY),
                   pl.BlockSpec((None, H, tm, dh), lambda i: (i // n_t, 0, i % n_t, 0))],
        out_shape=[jax.ShapeDtypeStruct(shape, F32), jax.ShapeDtypeStruct((B, H, T, dh), BF16)],
        scratch_shapes=[pltpu.VMEM((2, tm, D), F32), pltpu.SemaphoreType.DMA((2,)),
                        pltpu.SemaphoreType.DMA((max(n_e, 1),))],
        compiler_params=_params("arbitrary"),
        name=name,
    )(h, gamma.reshape(1, D).astype(F32), w, *earlier)


def _cumsum_lanes(x):
    R, L = x.shape
    assert L % LANES == 0

    def kernel(x_ref, o_ref):
        upper = (lax.broadcasted_iota(jnp.int32, (LANES, LANES), 0)
                 <= lax.broadcasted_iota(jnp.int32, (LANES, LANES), 1)).astype(BF16)
        carry = jnp.zeros((R, 1), F32)
        for c in range(L // LANES):
            sl = slice(c * LANES, (c + 1) * LANES)
            loc = _dot_ones_rhs(x_ref[:, sl], upper)
            o_ref[:, sl] = loc + carry
            carry = carry + loc[:, LANES - 1:LANES]

    return pl.pallas_call(
        kernel,
        out_shape=jax.ShapeDtypeStruct((R, L), F32),
        compiler_params=pltpu.CompilerParams(vmem_limit_bytes=VMEM_LIMIT_BYTES),
        name="fox_cumsum",
    )(x)


def _fox_attention(q, k, v, cq, ck, *, q_off, tq, tk):
    B, H, Tq, dh = q.shape
    Tk = k.shape[2]
    nq, nk = Tq // tq, Tk // tk
    assert Tq % tq == 0 and Tk % tk == 0 and tk % LANES == 0 and dh == LANES
    n_c = tk // LANES
    unroll = next(u for u in (16, 8, 4, 2, 1) if H % u == 0)

    def last_k(qi):
        return jnp.minimum(nk - 1, (q_off + (qi + 1) * tq - 1) // tk)

    def kernel(q_ref, k_ref, v_ref, cq_ref, ck_ref, o_ref, m_ref, acc_ref, cqs_ref):
        qi = pl.program_id(1)
        ki = pl.program_id(2)

        @pl.when(ki == 0)
        def _():
            m_ref[...] = jnp.full(m_ref.shape, NEG_INF, F32)
            acc_ref[...] = jnp.zeros(acc_ref.shape, F32)
            cqv = cq_ref[...] * LOG2E
            for h in range(H):
                cqs_ref[h] = jnp.broadcast_to(cqv[:, h:h + 1], (tq, LANES))

        q_lo = q_off + qi * tq
        k_lo = ki * tk
        active = k_lo <= q_lo + tq - 1
        crosses = k_lo + tk - 1 > q_lo

        def one_head(h, masked):
            s = lax.dot_general(q_ref[h], k_ref[h], _NT, preferred_element_type=F32)
            cqh = cqs_ref[h]
            ckh = ck_ref[pl.ds(h, 1), :] * LOG2E
            if masked:
                rel = (lax.broadcasted_iota(jnp.int32, (tq, LANES), 1)
                       - lax.broadcasted_iota(jnp.int32, (tq, LANES), 0))
            chunks = []
            for c in range(n_c):
                sl = slice(c * LANES, (c + 1) * LANES)
                sc = s[:, sl] - ckh[:, sl]
                if masked:
                    sc = jnp.where(rel <= q_lo - k_lo - c * LANES, sc, NEG_INF)
                chunks.append(sc)
            mx = functools.reduce(jnp.maximum, chunks)
            m_prev = m_ref[h]
            m_new = jnp.maximum(m_prev, jnp.max(mx, axis=1, keepdims=True) + cqh)
            alpha = jnp.exp2(m_prev - m_new)
            shift = m_new - cqh
            p = jnp.concatenate([jnp.exp2(sc - shift).astype(BF16) for sc in chunks], axis=1)
            v_ones = jnp.concatenate([v_ref[h], jnp.ones((tk, dh), BF16)], axis=1)
            pv = jnp.dot(p, v_ones, preferred_element_type=F32)
            acc_ref[h] = acc_ref[h] * jnp.concatenate([alpha, alpha], axis=1) + pv
            m_ref[h] = m_new

        def heads(masked):
            def body(i, carry):
                for u in range(unroll):
                    one_head(i * unroll + u, masked)
                return carry
            lax.fori_loop(0, H // unroll, body, 0)

        @pl.when(jnp.logical_and(active, crosses))
        def _():
            heads(True)

        @pl.when(jnp.logical_and(active, jnp.logical_not(crosses)))
        def _():
            heads(False)

        @pl.when(ki == nk - 1)
        def _():
            a = acc_ref[...]
            o_ref[...] = (a[:, :, :dh] / a[:, :, dh:]).astype(o_ref.dtype)

    kv_spec = pl.BlockSpec((None, H, tk, dh),
                           lambda b, qi, ki: (b, 0, jnp.minimum(ki, last_k(qi)), 0))
    return pl.pallas_call(
        kernel,
        grid=(B, nq, nk),
        in_specs=[
            pl.BlockSpec((None, H, tq, dh), lambda b, qi, ki: (b, 0, qi, 0)),
            kv_spec, kv_spec,
            pl.BlockSpec((None, tq, H), lambda b, qi, ki: (b, qi, 0)),
            pl.BlockSpec((None, H, tk), lambda b, qi, ki: (b, 0, jnp.minimum(ki, last_k(qi)))),
        ],
        out_specs=pl.BlockSpec((None, H, tq, dh), lambda b, qi, ki: (b, 0, qi, 0)),
        out_shape=jax.ShapeDtypeStruct((B, H, Tq, dh), BF16),
        scratch_shapes=[pltpu.VMEM((H, tq, LANES), F32), pltpu.VMEM((H, tq, 2 * dh), F32),
                        pltpu.VMEM((H, tq, LANES), F32)],
        compiler_params=_params("arbitrary", "arbitrary", "arbitrary"),
        name="fox_attention",
    )(q, k, v, cq, ck)


def _to_head_major(x):
    return jnp.transpose(x, (0, 2, 1, 3)).astype(BF16)


def _fox_layer(h, gamma, wts, earlier, stack, *, B, T, cache=None):
    wq, wk, wv, wf, bf, wo = wts
    M, D = h.shape
    H = bf.shape[0]
    q_hm, k32, k_hm, v32, v_hm, logf = _fox_project(h, gamma, wq, wk, wv, wf, bf, earlier, stack,
                                                    B=B, T=T, H=H)
    logf_bth = logf.reshape(B, T, H)
    if cache is None:
        P = 0
        logf_all = logf_bth
    else:
        ck_, cv_, cl_ = cache
        P = ck_.shape[1]
        k_hm = jnp.concatenate([_to_head_major(ck_), k_hm], axis=2)
        v_hm = jnp.concatenate([_to_head_major(cv_), v_hm], axis=2)
        logf_all = jnp.concatenate([cl_.astype(F32), logf_bth], axis=1)
    Tk = P + T
    tk = _tile(Tk, 512, LANES) if Tk % LANES == 0 else -(-Tk // LANES) * LANES
    pad = -Tk % tk
    if pad:
        k_hm = jnp.pad(k_hm, ((0, 0), (0, 0), (0, pad), (0, 0)))
        v_hm = jnp.pad(v_hm, ((0, 0), (0, 0), (0, pad), (0, 0)))
        logf_all = jnp.pad(logf_all, ((0, 0), (0, pad), (0, 0)))
    rows = jnp.transpose(logf_all, (0, 2, 1)).reshape(B * H, Tk + pad)
    ck = _cumsum_lanes(rows).reshape(B, H, Tk + pad)
    cq = jnp.transpose(ck[:, :, P:P + T], (0, 2, 1))
    o_hm = _fox_attention(q_hm, k_hm, v_hm, cq, ck, q_off=P, tq=_tile(T, 512), tk=tk)
    h = _out_proj(o_hm, wo, h, T=T, head_major=True, name="fox_out_proj")
    return h, k32, v32, logf_bth


def _out_proj(x, w, h, *, T, bias=None, head_major=False, name):
    M, D = h.shape
    tm = _tile(T if head_major else M, 512, 2 * SUBLANES)

    def epilogue(j, acc, ctx):
        ctx["outs"][0][...] = ctx["extras"][0][...] + acc

    return _proj(x, w, tm=tm, bias=bias, head_major_x=head_major,
                 extras=[(h, (tm, D), lambda i, j: (i, 0))],
                 outs=[_row_out(M, D, tm, F32)], epilogue=epilogue, name=name)[0]


def _ffn(h, gamma, w_up, w_down):
    M, D = h.shape
    F = w_up.shape[1]
    tm = _tile(M, 512)
    tf = _tile(F, 1024, LANES)

    def kernel(x_ref, g_ref, wu_ref, wd_ref, o_ref, xn_ref):
        @pl.when(pl.program_id(1) == 0)
        def _():
            x = x_ref[...]
            xn_ref[...] = _rms(x, g_ref[...]).astype(BF16)
            o_ref[...] = x
        hid = jnp.dot(xn_ref[...], wu_ref[...], preferred_element_type=F32)
        hid = jnp.square(jnp.maximum(hid, 0.0)).astype(BF16)
        o_ref[...] += jnp.dot(hid, wd_ref[...], preferred_element_type=F32)

    return pl.pallas_call(
        kernel,
        grid=(M // tm, F // tf),
        in_specs=[pl.BlockSpec((tm, D), lambda i, f: (i, 0)),
                  pl.BlockSpec((1, D), lambda i, f: (0, 0)),
                  pl.BlockSpec((D, tf), lambda i, f: (0, f)),
                  pl.BlockSpec((tf, D), lambda i, f: (f, 0))],
        out_specs=pl.BlockSpec((tm, D), lambda i, f: (i, 0)),
        out_shape=jax.ShapeDtypeStruct((M, D), F32),
        scratch_shapes=[pltpu.VMEM((tm, D), BF16)],
        compiler_params=_params("arbitrary", "arbitrary"),
        name="sq_relu_mlp",
    )(h, gamma.reshape(1, D).astype(F32), w_up, w_down)


def _ple(h, p, layer, gamma, w_ple, w_pg, *, final_gamma=None):
    M, D = h.shape
    E = p.shape[2]
    tm = _tile(M, 256)
    fin = final_gamma is not None

    def epilogue(j, acc, ctx):
        p_ref, wple_ref = ctx["extras"][:2]
        e = jnp.dot(p_ref[...].astype(BF16), wple_ref[...], preferred_element_type=F32)
        out = ctx["x"][...] + e * jax.nn.sigmoid(acc)
        if fin:
            out = _rms(out, ctx["extras"][2][...])
        ctx["outs"][0][...] = out

    extras = [(p, (None, tm, E), lambda i, j: (layer, i, 0)),
              (w_ple, (E, D), lambda i, j: (0, 0))]
    if fin:
        extras.append((final_gamma.reshape(1, D).astype(F32), (1, D), lambda i, j: (0, 0)))
    return _proj(h, w_pg, tm=tm, gamma=gamma, extras=extras,
                 outs=[_row_out(M, D, tm, F32)], epilogue=epilogue, name="ple_add")[0]


def _hgrn_level_codes(C):
    t = np.arange(C)[:, None]
    s = np.arange(C)[None, :]
    code = np.where((t // HGRN_BLOCK == s // HGRN_BLOCK) & (s <= t), 1, 0)
    level, R = 2, HGRN_BLOCK
    while R < C:
        code = np.where(((t // R) % 2 == 1) & (s // R == t // R - 1), level, code)
        level, R = level + 1, 2 * R
    return jnp.asarray(code, jnp.int32)


def _hgrn_scan(q, fl, iv, g, lb, gn, s0, *, B, T, tc, C):
    M, D = q.shape
    H, dk, dv = s0.shape[1:]
    assert dk == LANES and dv == LANES and T % tc == 0 and tc % C == 0
    assert C % HGRN_BLOCK == 0 and (C // HGRN_BLOCK) & (C // HGRN_BLOCK - 1) == 0
    n_t = T // tc
    steps = [1 << i for i in range(HGRN_BLOCK.bit_length() - 1)]

    def kernel(q_ref, fl_ref, iv_ref, g_ref, lb_ref, gn_ref, s0_ref, code_ref, o_ref, sf_ref,
               st_ref):
        t = pl.program_id(2)

        @pl.when(t == 0)
        def _():
            st_ref[...] = s0_ref[...].T

        lbv = lb_ref[...]
        gnv = gn_ref[...]
        code = code_ref[...]
        row = lax.broadcasted_iota(jnp.int32, (C, LANES), 0)
        pos = jnp.bitwise_and(row, HGRN_BLOCK - 1)
        st = st_ref[...]
        for c in range(tc // C):
            rs = slice(c * C, (c + 1) * C)
            qv = q_ref[rs, :]
            vv = iv_ref[rs, :]
            f = lbv + (1.0 - lbv) * jax.nn.sigmoid(fl_ref[rs, :])
            kk = 1.0 - f
            L = jnp.log(f)
            for sh in steps:
                L = L + jnp.where(pos >= sh, pltpu.roll(L, sh, 0), 0.0)
            Tt = jnp.where(pos == HGRN_BLOCK - 1, L, 0.0)
            for sh in steps:
                Tt = Tt + pltpu.roll(Tt, C - sh, 0)
            a = lax.dot_general((qv * jnp.exp(L)).astype(BF16), (kk * jnp.exp(-L)).astype(BF16),
                                _NT, preferred_element_type=F32)
            A = jnp.where(code == 1, a, 0.0)
            level, R = 2, HGRN_BLOCK
            while R < C:
                upper = jnp.bitwise_and(row, R) != 0
                x = (jnp.where(upper, qv, kk)
                     * jnp.exp(jnp.where(upper, L, Tt - L))).astype(BF16)
                a = lax.dot_general(x, x, _NT, preferred_element_type=F32)
                A = jnp.where(code == level, a, A)
                below = pltpu.roll(Tt, R, 0)
                above = pltpu.roll(Tt, C - R, 0)
                L = L + jnp.where(upper, below, 0.0)
                Tt = Tt + jnp.where(upper, below, above)
                level, R = level + 1, 2 * R
            o = jnp.dot(A.astype(BF16), vv, preferred_element_type=F32)
            o = o + lax.dot_general((qv * jnp.exp(L)).astype(BF16), st.astype(BF16), _NT,
                                    preferred_element_type=F32)
            k_end = (kk * jnp.exp(Tt - L)).astype(BF16)
            st = st * jnp.exp(Tt[0:1, :]) + lax.dot_general(vv, k_end, _TN,
                                                            preferred_element_type=F32)
            o = _rms(o, gnv) * jax.nn.sigmoid(g_ref[rs, :])
            o_ref[rs, :] = o.astype(o_ref.dtype)
        st_ref[...] = st

        @pl.when(t == n_t - 1)
        def _():
            sf_ref[...] = st.T

    blk = pl.BlockSpec((tc, LANES), lambda b, h, t: (b * n_t + t, h))
    vec = pl.BlockSpec((1, LANES), lambda b, h, t: (0, h))
    st_spec = pl.BlockSpec((None, None, dk, dv), lambda b, h, t: (b, h, 0, 0))
    return pl.pallas_call(
        kernel,
        grid=(B, H, n_t),
        in_specs=[blk, blk, blk, blk, vec, vec, st_spec,
                  pl.BlockSpec((C, C), lambda b, h, t: (0, 0))],
        out_specs=[blk, st_spec],
        out_shape=[jax.ShapeDtypeStruct((M, D), BF16), jax.ShapeDtypeStruct(s0.shape, F32)],
        scratch_shapes=[pltpu.VMEM((dv, dk), F32)],
        compiler_params=_params("arbitrary", "arbitrary", "arbitrary"),
        name="hgrn_scan",
    )(q, fl, iv, g, lb.reshape(1, D).astype(F32), gn.reshape(1, D).astype(F32), s0,
      _hgrn_level_codes(C))


def _hgrn_layer(h, gamma, wts, lb, s0, *, B, T):
    w_parts, gn, wo = wts
    M, D = h.shape
    tm = _tile(M, 512, 2 * SUBLANES)

    def cast_epilogue(j, acc, ctx):
        ctx["outs"][0][...] = acc.astype(ctx["outs"][0].dtype)

    parts = []
    for w, dt, nm in zip(w_parts, (F32, F32, BF16, F32), ("q", "f", "i", "g")):
        parts.append(_proj(h, w, tm=tm, gamma=gamma, outs=[_row_out(M, D, tm, dt)],
                           epilogue=cast_epilogue, name="hgrn_%s_proj" % nm)[0])
    C = LANES
    while T % C:
        C //= 2
    o, s_new = _hgrn_scan(*parts, lb, gn, s0, B=B, T=T, tc=_tile(T, 1024, C), C=C)
    h = _out_proj(o, wo, h, T=T, name="hgrn_out_proj")
    return h, s_new


def _conv_ln_silu(u, buf, w_dw, b_dw, ln_g, ln_b, *, B, T, tc):
    M, D = u.shape
    W = w_dw.shape[0]
    halo = buf.shape[1]
    off = halo - (W - 1)
    n_t = T // tc
    assert T % tc == 0 and tc >= halo and off >= 0
    cw = _tile(D, 4 * LANES, LANES)
    n_groups = -(-W // SUBLANES)

    def kernel(u_ref, buf_ref, w_ref, bd_ref, lg_ref, lbias_ref, y_ref, ext_ref, acc_ref,
               sh_ref):
        @pl.when(pl.program_id(1) == 0)
        def _():
            ext_ref[0:halo, :] = buf_ref[...]
        ext_ref[halo:halo + tc, :] = u_ref[...]
        for cb in range(D // cw):
            cols = slice(cb * cw, (cb + 1) * cw)
            part = jnp.zeros((tc, cw), F32) + bd_ref[:, cols]
            for r in range(min(SUBLANES, W)):
                taps = range(r, W, SUBLANES)
                span = tc + (len(taps) - 1) * SUBLANES
                sh_ref[0:span, :] = ext_ref[off + r:off + r + span, cols]
                for m, j in enumerate(taps):
                    part = part + (w_ref[j:j + 1, cols]
                                   * sh_ref[m * SUBLANES:m * SUBLANES + tc, :])
            acc_ref[:, cols] = part
        acc = acc_ref[...]
        mu = jnp.mean(acc, axis=-1, keepdims=True)
        d = acc - mu
        var = jnp.mean(d * d, axis=-1, keepdims=True)
        y = d * lax.rsqrt(var + EPS) * lg_ref[...] + lbias_ref[...]
        y_ref[...] = (y * jax.nn.sigmoid(y)).astype(y_ref.dtype)
        ext_ref[0:halo, :] = ext_ref[tc:tc + halo, :]

    vec = pl.BlockSpec((1, D), lambda b, t: (0, 0))
    return pl.pallas_call(
        kernel,
        grid=(B, n_t),
        in_specs=[pl.BlockSpec((tc, D), lambda b, t: (b * n_t + t, 0)),
                  pl.BlockSpec((None, halo, D), lambda b, t: (b, 0, 0)),
                  pl.BlockSpec((W, D), lambda b, t: (0, 0)), vec, vec, vec],
        out_specs=pl.BlockSpec((tc, D), lambda b, t: (b * n_t + t, 0)),
        out_shape=jax.ShapeDtypeStruct((M, D), BF16),
        scratch_shapes=[pltpu.VMEM((halo + tc, D), F32), pltpu.VMEM((tc, D), F32),
                        pltpu.VMEM((tc + (n_groups - 1) * SUBLANES, cw), F32)],
        compiler_params=_params("arbitrary", "arbitrary"),
        name="conv_ln_silu",
    )(u, buf, w_dw.astype(F32), b_dw.reshape(1, D).astype(F32),
      ln_g.reshape(1, D).astype(F32), ln_b.reshape(1, D).astype(F32))


def _conv_layer(h, gamma, wts, buf, *, B, T):
    w_glu, b_glu, w_dw, b_dw, ln_g, ln_b, w_pw2, b_pw2 = wts
    M, D = h.shape
    W = w_dw.shape[0]
    assert T >= W - 1
    tm = _tile(M, 512)
    wt = w_glu.shape[1] // 2

    def glu_epilogue(j, acc, ctx):
        half = acc.shape[1] // 2
        ctx["outs"][0][...] = acc[:, :half] * jax.nn.sigmoid(acc[:, half:])

    u = _proj(h, w_glu, tm=tm, wt=wt, gamma=gamma, bias=b_glu,
              outs=[_row_out(M, D, tm, F32, wt=wt // 2)], epilogue=glu_epilogue,
              name="conv_glu_proj")[0]
    halo = -(-(W - 1) // SUBLANES) * SUBLANES
    buf_p = jnp.pad(buf.astype(F32), ((0, 0), (halo - (W - 1), 0), (0, 0)))
    y = _conv_ln_silu(u, buf_p, w_dw, b_dw, ln_g, ln_b, B=B, T=T, tc=_tile(T, 256))
    h = _out_proj(y, w_pw2, h, T=T, bias=b_pw2, name="conv_out_proj")
    return h, u.reshape(B, T, D)[:, T - (W - 1):, :]


def _glu_weights(w, b, n_tiles=2):
    K, N2 = w.shape
    tn = N2 // 2 // n_tiles
    wr = w.reshape(K, 2, n_tiles, tn).transpose(0, 2, 1, 3).reshape(K, N2)
    br = b.reshape(2, n_tiles, tn).transpose(1, 0, 2).reshape(N2)
    return wr, br


def kernel(x_prompt, x_sample, p_prompt, p_sample, cache_fox_k, cache_fox_v, cache_fox_logf, state_hgrn, cache_conv, norm_mix, norm_ffn, norm_ple, norm_final, w_in_a, b_f_a, w_o_a, w_in_b, lb_logits, gnorm_b, w_o_b, w_pw1, b_pw1, w_dw, b_dw, ln_g, ln_b, w_pw2, b_pw2, w_up, w_down, w_ple, w_pg):
    depth, D = norm_mix.shape
    n_mixers = 3
    lb_soft = jax.nn.softmax(lb_logits.astype(F32), axis=0)
    lower_bounds = jnp.cumsum(lb_soft, axis=0) - lb_soft[0]

    groups = []
    for x, p in ((x_prompt, p_prompt), (x_sample, p_sample)):
        B, T, _ = x.shape
        groups.append(dict(B=B, T=T, h=x.reshape(B * T, D), p=p.reshape(depth, B * T, -1)))
    is_sample = (False, True)

    fox_k, fox_v, fox_l = [[], []], [[], []], ([], [])
    hg_state, conv_buf = ([], []), ([], [])
    n_fox = len(range(0, depth, n_mixers))
    ia = ib = ic = 0
    for i in range(depth):
        kind = i % n_mixers
        if kind == 0:
            w = w_in_a[ia].astype(BF16)
            wts = (w[:, :D], w[:, D:2 * D], w[:, 2 * D:3 * D], w[:, 3 * D:], b_f_a[ia],
                   w_o_a[ia].astype(BF16))
        elif kind == 1:
            w = w_in_b[ib].astype(BF16)
            wts = (tuple(w[:, n * D:(n + 1) * D] for n in range(4)), gnorm_b[ib],
                   w_o_b[ib].astype(BF16))
        else:
            wts = _glu_weights(w_pw1[ic].astype(BF16), b_pw1[ic]) + (
                w_dw[ic], b_dw[ic], ln_g[ic], ln_b[ic], w_pw2[ic].astype(BF16), b_pw2[ic])
        wu, wd = w_up[i].astype(BF16), w_down[i].astype(BF16)
        wple, wpg = w_ple[i].astype(BF16), w_pg[i].astype(BF16)
        for gi, grp in enumerate(groups):
            B, T, h = grp["B"], grp["T"], grp["h"]
            if kind == 0:
                cache = ((cache_fox_k[ia], cache_fox_v[ia], cache_fox_logf[ia])
                         if is_sample[gi] else None)
                stack = ia == n_fox - 1
                earlier = (tuple(fox_k[gi]), tuple(fox_v[gi])) if stack else ((), ())
                h, k_new, v_new, l_new = _fox_layer(h, norm_mix[i], wts, earlier, stack,
                                                    B=B, T=T, cache=cache)
                if stack:
                    fox_k[gi], fox_v[gi] = k_new, v_new
                else:
                    fox_k[gi].append(k_new)
                    fox_v[gi].append(v_new)
                fox_l[gi].append(l_new)
            elif kind == 1:
                s0 = (state_hgrn[ib].astype(F32) if is_sample[gi]
                      else jnp.zeros((B,) + state_hgrn.shape[2:], F32))
                h, s_new = _hgrn_layer(h, norm_mix[i], wts, lower_bounds[i], s0, B=B, T=T)
                hg_state[gi].append(s_new)
            else:
                buf = (cache_conv[ic] if is_sample[gi]
                       else jnp.zeros((B, w_dw.shape[1] - 1, D), F32))
                h, b_new = _conv_layer(h, norm_mix[i], wts, buf, B=B, T=T)
                conv_buf[gi].append(b_new)
            h = _ffn(h, norm_ffn[i], wu, wd)
            h = _ple(h, grp["p"], i, norm_ple[i], wple, wpg,
                     final_gamma=norm_final if i == depth - 1 else None)
            grp["h"] = h
        if kind == 0:
            ia += 1
        elif kind == 1:
            ib += 1
        else:
            ic += 1

    ys = [grp["h"].reshape(grp["B"], grp["T"], D) for grp in groups]
    return (ys[0], ys[1],
            fox_k[0], fox_v[0], jnp.stack(fox_l[0]),
            fox_k[1], fox_v[1], jnp.stack(fox_l[1]),
            jnp.stack(hg_state[0]), jnp.stack(hg_state[1]),
            jnp.stack(conv_buf[0]), jnp.stack(conv_buf[1]))
```

```python
import functools

import jax
import jax.numpy as jnp
import numpy as np
from jax import lax
from jax.experimental import pallas as pl
from jax.experimental.pallas import tpu as pltpu

F32 = jnp.float32
BF16 = jnp.bfloat16
EPS = 1e-6
NEG_INF = -1e30
LOG2E = 1.4426950408889634
HGRN_BLOCK = 16
LANES = 128
SUBLANES = 8
VMEM_LIMIT_BYTES = 56 * 1024 * 1024

_NT = (((1,), (1,)), ((), ()))
_TN = (((0,), (0,)), ((), ()))


def _tile(n, pref, mult=SUBLANES):
    for d in range(min(n, pref), 0, -1):
        if n % d == 0 and d % mult == 0:
            return d
    return n


def _params(*sem):
    return pltpu.CompilerParams(dimension_semantics=sem, vmem_limit_bytes=VMEM_LIMIT_BYTES)


def _rms(x, g):
    return x * lax.rsqrt(jnp.mean(x * x, axis=-1, keepdims=True) + EPS) * g


def _split3(x):
    hi = x.astype(BF16)
    r1 = x - hi.astype(F32)
    mid = r1.astype(BF16)
    lo = (r1 - mid.astype(F32)).astype(BF16)
    return hi, mid, lo


def _dot_ones_rhs(x, mask_bf16):
    out = None
    for part in _split3(x):
        d = jnp.dot(part, mask_bf16, preferred_element_type=F32)
        out = d if out is None else out + d
    return out


def _proj(x, w, *, tm, wt=None, gamma=None, bias=None, head_major_x=False,
          extras=(), outs, epilogue, name):
    K, N = w.shape
    wt = N if wt is None else wt
    if head_major_x:
        B, H, T, dh = x.shape
        M = B * T
        n_t = T // tm
        x_spec = pl.BlockSpec((None, H, tm, dh), lambda i, j: (i // n_t, 0, i % n_t, 0))
    else:
        M = x.shape[0]
        x_spec = pl.BlockSpec((tm, K), lambda i, j: (i, 0))
    has_g = gamma is not None
    has_b = bias is not None
    assert M % tm == 0 and N % wt == 0
    assert has_g == (x.dtype == F32)

    def kernel(*refs):
        it = iter(refs)
        x_ref = next(it)
        g_ref = next(it) if has_g else None
        w_ref = next(it)
        b_ref = next(it) if has_b else None
        extra_refs = [next(it) for _ in extras]
        out_refs = [next(it) for _ in outs]
        xn_ref = next(it) if has_g else None
        j = pl.program_id(1)
        if has_g:
            @pl.when(j == 0)
            def _():
                xn_ref[...] = _rms(x_ref[...], g_ref[...]).astype(BF16)
            lhs = xn_ref[...]
        elif head_major_x:
            lhs = jnp.concatenate([x_ref[h] for h in range(x_ref.shape[0])], axis=1)
        else:
            lhs = x_ref[...]
        acc = jnp.dot(lhs, w_ref[...], preferred_element_type=F32)
        if has_b:
            acc = acc + b_ref[...]
        epilogue(j, acc, dict(x=x_ref, lhs=lhs, extras=extra_refs, outs=out_refs))

    in_arrays = [x]
    in_specs = [x_spec]
    if has_g:
        in_arrays.append(gamma.reshape(1, K).astype(F32))
        in_specs.append(pl.BlockSpec((1, K), lambda i, j: (0, 0)))
    in_arrays.append(w)
    in_specs.append(pl.BlockSpec((K, wt), lambda i, j: (0, j)))
    if has_b:
        in_arrays.append(bias.reshape(1, N).astype(F32))
        in_specs.append(pl.BlockSpec((1, wt), lambda i, j: (0, j)))
    for arr, blk, imap in extras:
        in_arrays.append(arr)
        in_specs.append(pl.BlockSpec(blk, imap))
    return pl.pallas_call(
        kernel,
        grid=(M // tm, N // wt),
        in_specs=in_specs,
        out_specs=[pl.BlockSpec(blk, imap) for _, blk, imap in outs],
        out_shape=[sds for sds, _, _ in outs],
        scratch_shapes=[pltpu.VMEM((tm, K), BF16)] if has_g else [],
        compiler_params=_params("arbitrary", "arbitrary"),
        name=name,
    )(*in_arrays)


def _store_head_major(ref, acc):
    for h in range(ref.shape[0]):
        ref[h] = acc[:, h * ref.shape[2]:(h + 1) * ref.shape[2]].astype(ref.dtype)


def _head_major_out(B, H, T, dh, tm):
    n_t = T // tm
    return (jax.ShapeDtypeStruct((B, H, T, dh), BF16), (None, H, tm, dh),
            lambda i, j: (i // n_t, 0, i % n_t, 0))


def _row_out(M, N, tm, dtype, wt=None):
    wt = N if wt is None else wt
    return (jax.ShapeDtypeStruct((M, N), dtype), (tm, wt), lambda i, j: (i, j))


def _log_sigmoid(x):
    return jnp.minimum(x, 0.0) - jnp.log(1.0 + jnp.exp(-jnp.abs(x)))


def _fox_project(h, gamma, wq, wk, wv, wf, bf, earlier, stack, *, B, T, H):
    M, D = h.shape
    dh = D // H
    tm = _tile(T, 512)
    scale = dh ** -0.5 * LOG2E

    def q_epilogue(j, acc, ctx):
        q_ref, logf_ref = ctx["outs"]
        wf_ref, bf_ref = ctx["extras"]
        _store_head_major(q_ref, acc * scale)
        fl = jnp.dot(ctx["lhs"], wf_ref[...], preferred_element_type=F32)
        logf_ref[...] = _log_sigmoid(fl[:, :H] + bf_ref[...])

    wf_pad = jnp.pad(wf, ((0, 0), (0, LANES - H)))
    q_hm, logf = _proj(
        h, wq, tm=tm, gamma=gamma,
        extras=[(wf_pad, (D, LANES), lambda i, j: (0, 0)),
                (bf.reshape(1, H).astype(F32), (1, H), lambda i, j: (0, 0))],
        outs=[_head_major_out(B, H, T, dh, tm), _row_out(M, H, tm, F32)],
        epilogue=q_epilogue, name="fox_q_proj")

    k32, k_hm = _fox_kv_proj(h, gamma, wk, earlier[0], stack, B=B, T=T, H=H, tm=tm,
                             name="fox_k_proj")
    v32, v_hm = _fox_kv_proj(h, gamma, wv, earlier[1], stack, B=B, T=T, H=H, tm=tm,
                             name="fox_v_proj")
    return q_hm, k32, k_hm, v32, v_hm, logf


def _fox_kv_proj(h, gamma, w, earlier, stack, *, B, T, H, tm, name):
    M, D = h.shape
    dh = D // H
    n_t = T // tm
    n = M // tm
    n_e = len(earlier)
    assert stack or not n_e

    def kernel(x_ref, g_ref, w_ref, *rest):
        earlier_refs = rest[:n_e]
        full_ref, hm_ref, stage_ref, sem_ref, esem_ref = rest[n_e:]
        own = full_ref.at[n_e] if stack else full_ref
        i = pl.program_id(0)
        slot = i % 2

        def head_copy(step, slot, h):
            t0 = pl.multiple_of((step % n_t) * tm, tm)
            return pltpu.make_async_copy(stage_ref.at[slot, :, pl.ds(h * dh, dh)],
                                         own.at[step // n_t, pl.ds(t0, tm), h, :],
                                         sem_ref.at[slot])

        def earlier_copy(e):
            t0 = pl.multiple_of((i % n_t) * tm, tm)
            return pltpu.make_async_copy(earlier_refs[e],
                                         full_ref.at[e, pl.ds(i // n_t, 1), pl.ds(t0, tm)],
                                         esem_ref.at[e])

        for e in range(n_e):
            earlier_copy(e).start()

        @pl.when(i >= 2)
        def _():
            for hd in range(H):
                head_copy(i - 2, slot, hd).wait()

        acc = jnp.dot(_rms(x_ref[...], g_ref[...]).astype(BF16), w_ref[...],
                      preferred_element_type=F32)
        stage_ref[slot] = acc
        _store_head_major(hm_ref, acc)
        for hd in range(H):
            head_copy(i, slot, hd).start()
        for e in range(n_e):
            earlier_copy(e).wait()

        @pl.when(i == n - 1)
        def _():
            if n >= 2:
                for hd in range(H):
                    head_copy(i - 1, 1 - slot, hd).wait()
            for hd in range(H):
                head_copy(i, slot, hd).wait()

    shape = (n_e + 1, B, T, H, dh) if stack else (B, T, H, dh)
    return pl.pallas_call(
        kernel,
        grid=(n,),
        in_specs=[pl.BlockSpec((tm, D), lambda i: (i, 0)),
                  pl.BlockSpec((1, D), lambda i: (0, 0)),
                  pl.BlockSpec((D, D), lambda i: (0, 0))]
        + [pl.BlockSpec((1, tm, H, dh), lambda i: (i // n_t, i % n_t, 0, 0))] * n_e,
        out_specs=[pl.BlockSpec(memory_space=pl.ANY),
                   pl.BlockSpec((None, H, tm, dh), lambda i: (i // n_t, 0, i % n_t, 0))],
        out_shape=[jax.ShapeDtypeStruct(shape, F32), jax.ShapeDtypeStruct((B, H, T, dh), BF16)],
        scratch_shapes=[pltpu.VMEM((2, tm, D), F32), pltpu.SemaphoreType.DMA((2,)),
                        pltpu.SemaphoreType.DMA((max(n_e, 1),))],
        compiler_params=_params("arbitrary"),
        name=name,
    )(h, gamma.reshape(1, D).astype(F32), w, *earlier)


def _cumsum_lanes(x):
    R, L = x.shape
    assert L % LANES == 0

    def kernel(x_ref, o_ref):
        upper = (lax.broadcasted_iota(jnp.int32, (LANES, LANES), 0)
                 <= lax.broadcasted_iota(jnp.int32, (LANES, LANES), 1)).astype(BF16)
        carry = jnp.zeros((R, 1), F32)
        for c in range(L // LANES):
            sl = slice(c * LANES, (c + 1) * LANES)
            loc = _dot_ones_rhs(x_ref[:, sl], upper)
            o_ref[:, sl] = loc + carry
            carry = carry + loc[:, LANES - 1:LANES]

    return pl.pallas_call(
        kernel,
        out_shape=jax.ShapeDtypeStruct((R, L), F32),
        compiler_params=pltpu.CompilerParams(vmem_limit_bytes=VMEM_LIMIT_BYTES),
        name="fox_cumsum",
    )(x)


def _fox_attention(q, k, v, cq, ck, *, q_off, tq, tk):
    B, H, Tq, dh = q.shape
    Tk = k.shape[2]
    nq, nk = Tq // tq, Tk // tk
    assert Tq % tq == 0 and Tk % tk == 0 and tk % LANES == 0 and dh == LANES
    n_c = tk // LANES
    unroll = next(u for u in (16, 8, 4, 2, 1) if H % u == 0)

    def last_k(qi):
        return jnp.minimum(nk - 1, (q_off + (qi + 1) * tq - 1) // tk)

    def kernel(q_ref, k_ref, v_ref, cq_ref, ck_ref, o_ref, m_ref, acc_ref, cqs_ref):
        qi = pl.program_id(1)
        ki = pl.program_id(2)

        @pl.when(ki == 0)
        def _():
            m_ref[...] = jnp.full(m_ref.shape, NEG_INF, F32)
            acc_ref[...] = jnp.zeros(acc_ref.shape, F32)
            cqv = cq_ref[...] * LOG2E
            for h in range(H):
                cqs_ref[h] = jnp.broadcast_to(cqv[:, h:h + 1], (tq, LANES))

        q_lo = q_off + qi * tq
        k_lo = ki * tk
        active = k_lo <= q_lo + tq - 1
        crosses = k_lo + tk - 1 > q_lo

        def one_head(h, masked):
            s = lax.dot_general(q_ref[h], k_ref[h], _NT, preferred_element_type=F32)
            cqh = cqs_ref[h]
            ckh = ck_ref[pl.ds(h, 1), :] * LOG2E
            if masked:
                rel = (lax.broadcasted_iota(jnp.int32, (tq, LANES), 1)
                       - lax.broadcasted_iota(jnp.int32, (tq, LANES), 0))
            chunks = []
            for c in range(n_c):
                sl = slice(c * LANES, (c + 1) * LANES)
                sc = s[:, sl] - ckh[:, sl]
                if masked:
                    sc = jnp.where(rel <= q_lo - k_lo - c * LANES, sc, NEG_INF)
                chunks.append(sc)
            mx = functools.reduce(jnp.maximum, chunks)
            m_prev = m_ref[h]
            m_new = jnp.maximum(m_prev, jnp.max(mx, axis=1, keepdims=True) + cqh)
            alpha = jnp.exp2(m_prev - m_new)
            shift = m_new - cqh
            p = jnp.concatenate([jnp.exp2(sc - shift).astype(BF16) for sc in chunks], axis=1)
            v_ones = jnp.concatenate([v_ref[h], jnp.ones((tk, dh), BF16)], axis=1)
            pv = jnp.dot(p, v_ones, preferred_element_type=F32)
            acc_ref[h] = acc_ref[h] * jnp.concatenate([alpha, alpha], axis=1) + pv
            m_ref[h] = m_new

        def heads(masked):
            def body(i, carry):
                for u in range(unroll):
                    one_head(i * unroll + u, masked)
                return carry
            lax.fori_loop(0, H // unroll, body, 0)

        @pl.when(jnp.logical_and(active, crosses))
        def _():
            heads(True)

        @pl.when(jnp.logical_and(active, jnp.logical_not(crosses)))
        def _():
            heads(False)

        @pl.when(ki == nk - 1)
        def _():
            a = acc_ref[...]
            o_ref[...] = (a[:, :, :dh] / a[:, :, dh:]).astype(o_ref.dtype)

    kv_spec = pl.BlockSpec((None, H, tk, dh),
                           lambda b, qi, ki: (b, 0, jnp.minimum(ki, last_k(qi)), 0))
    return pl.pallas_call(
        kernel,
        grid=(B, nq, nk),
        in_specs=[
            pl.BlockSpec((None, H, tq, dh), lambda b, qi, ki: (b, 0, qi, 0)),
            kv_spec, kv_spec,
            pl.BlockSpec((None, tq, H), lambda b, qi, ki: (b, qi, 0)),
            pl.BlockSpec((None, H, tk), lambda b, qi, ki: (b, 0, jnp.minimum(ki, last_k(qi)))),
        ],
        out_specs=pl.BlockSpec((None, H, tq, dh), lambda b, qi, ki: (b, 0, qi, 0)),
        out_shape=jax.ShapeDtypeStruct((B, H, Tq, dh), BF16),
        scratch_shapes=[pltpu.VMEM((H, tq, LANES), F32), pltpu.VMEM((H, tq, 2 * dh), F32),
                        pltpu.VMEM((H, tq, LANES), F32)],
        compiler_params=_params("arbitrary", "arbitrary", "arbitrary"),
        name="fox_attention",
    )(q, k, v, cq, ck)


def _to_head_major(x):
    return jnp.transpose(x, (0, 2, 1, 3)).astype(BF16)


def _fox_layer(h, gamma, wts, earlier, stack, *, B, T, cache=None):
    wq, wk, wv, wf, bf, wo = wts
    M, D = h.shape
    H = bf.shape[0]
    q_hm, k32, k_hm, v32, v_hm, logf = _fox_project(h, gamma, wq, wk, wv, wf, bf, earlier, stack,
                                                    B=B, T=T, H=H)
    logf_bth = logf.reshape(B, T, H)
    if cache is None:
        P = 0
        logf_all = logf_bth
    else:
        ck_, cv_, cl_ = cache
        P = ck_.shape[1]
        k_hm = jnp.concatenate([_to_head_major(ck_), k_hm], axis=2)
        v_hm = jnp.concatenate([_to_head_major(cv_), v_hm], axis=2)
        logf_all = jnp.concatenate([cl_.astype(F32), logf_bth], axis=1)
    Tk = P + T
    tk = _tile(Tk, 512, LANES) if Tk % LANES == 0 else -(-Tk // LANES) * LANES
    pad = -Tk % tk
    if pad:
        k_hm = jnp.pad(k_hm, ((0, 0), (0, 0), (0, pad), (0, 0)))
        v_hm = jnp.pad(v_hm, ((0, 0), (0, 0), (0, pad), (0, 0)))
        logf_all = jnp.pad(logf_all, ((0, 0), (0, pad), (0, 0)))
    rows = jnp.transpose(logf_all, (0, 2, 1)).reshape(B * H, Tk + pad)
    ck = _cumsum_lanes(rows).reshape(B, H, Tk + pad)
    cq = jnp.transpose(ck[:, :, P:P + T], (0, 2, 1))
    o_hm = _fox_attention(q_hm, k_hm, v_hm, cq, ck, q_off=P, tq=_tile(T, 512), tk=tk)
    h = _out_proj(o_hm, wo, h, T=T, head_major=True, name="fox_out_proj")
    return h, k32, v32, logf_bth


def _out_proj(x, w, h, *, T, bias=None, head_major=False, name):
    M, D = h.shape
    tm = _tile(T if head_major else M, 512, 2 * SUBLANES)

    def epilogue(j, acc, ctx):
        ctx["outs"][0][...] = ctx["extras"][0][...] + acc

    return _proj(x, w, tm=tm, bias=bias, head_major_x=head_major,
                 extras=[(h, (tm, D), lambda i, j: (i, 0))],
                 outs=[_row_out(M, D, tm, F32)], epilogue=epilogue, name=name)[0]


def _ffn(h, gamma, w_up, w_down):
    M, D = h.shape
    F = w_up.shape[1]
    tm = _tile(M, 512)
    tf = _tile(F, 1024, LANES)

    def kernel(x_ref, g_ref, wu_ref, wd_ref, o_ref, xn_ref):
        @pl.when(pl.program_id(1) == 0)
        def _():
            x = x_ref[...]
            xn_ref[...] = _rms(x, g_ref[...]).astype(BF16)
            o_ref[...] = x
        hid = jnp.dot(xn_ref[...], wu_ref[...], preferred_element_type=F32)
        hid = jnp.square(jnp.maximum(hid, 0.0)).astype(BF16)
        o_ref[...] += jnp.dot(hid, wd_ref[...], preferred_element_type=F32)

    return pl.pallas_call(
        kernel,
        grid=(M // tm, F // tf),
        in_specs=[pl.BlockSpec((tm, D), lambda i, f: (i, 0)),
                  pl.BlockSpec((1, D), lambda i, f: (0, 0)),
                  pl.BlockSpec((D, tf), lambda i, f: (0, f)),
                  pl.BlockSpec((tf, D), lambda i, f: (f, 0))],
        out_specs=pl.BlockSpec((tm, D), lambda i, f: (i, 0)),
        out_shape=jax.ShapeDtypeStruct((M, D), F32),
        scratch_shapes=[pltpu.VMEM((tm, D), BF16)],
        compiler_params=_params("arbitrary", "arbitrary"),
        name="sq_relu_mlp",
    )(h, gamma.reshape(1, D).astype(F32), w_up, w_down)


def _ple(h, p, layer, gamma, w_ple, w_pg, *, final_gamma=None):
    M, D = h.shape
    E = p.shape[2]
    tm = _tile(M, 256)
    fin = final_gamma is not None

    def epilogue(j, acc, ctx):
        p_ref, wple_ref = ctx["extras"][:2]
        e = jnp.dot(p_ref[...].astype(BF16), wple_ref[...], preferred_element_type=F32)
        out = ctx["x"][...] + e * jax.nn.sigmoid(acc)
        if fin:
            out = _rms(out, ctx["extras"][2][...])
        ctx["outs"][0][...] = out

    extras = [(p, (None, tm, E), lambda i, j: (layer, i, 0)),
              (w_ple, (E, D), lambda i, j: (0, 0))]
    if fin:
        extras.append((final_gamma.reshape(1, D).astype(F32), (1, D), lambda i, j: (0, 0)))
    return _proj(h, w_pg, tm=tm, gamma=gamma, extras=extras,
                 outs=[_row_out(M, D, tm, F32)], epilogue=epilogue, name="ple_add")[0]


def _hgrn_level_codes(C):
    t = np.arange(C)[:, None]
    s = np.arange(C)[None, :]
    code = np.where((t // HGRN_BLOCK == s // HGRN_BLOCK) & (s <= t), 1, 0)
    level, R = 2, HGRN_BLOCK
    while R < C:
        code = np.where(((t // R) % 2 == 1) & (s // R == t // R - 1), level, code)
        level, R = level + 1, 2 * R
    return jnp.asarray(code, jnp.int32)


def _hgrn_scan(q, fl, iv, g, lb, gn, s0, *, B, T, tc, C):
    M, D = q.shape
    H, dk, dv = s0.shape[1:]
    assert dk == LANES and dv == LANES and T % tc == 0 and tc % C == 0
    assert C % HGRN_BLOCK == 0 and (C // HGRN_BLOCK) & (C // HGRN_BLOCK - 1) == 0
    n_t = T // tc
    steps = [1 << i for i in range(HGRN_BLOCK.bit_length() - 1)]

    def kernel(q_ref, fl_ref, iv_ref, g_ref, lb_ref, gn_ref, s0_ref, code_ref, o_ref, sf_ref,
               st_ref):
        t = pl.program_id(2)

        @pl.when(t == 0)
        def _():
            st_ref[...] = s0_ref[...].T

        lbv = lb_ref[...]
        gnv = gn_ref[...]
        code = code_ref[...]
        row = lax.broadcasted_iota(jnp.int32, (C, LANES), 0)
        pos = jnp.bitwise_and(row, HGRN_BLOCK - 1)
        st = st_ref[...]
        for c in range(tc // C):
            rs = slice(c * C, (c + 1) * C)
            qv = q_ref[rs, :]
            vv = iv_ref[rs, :]
            f = lbv + (1.0 - lbv) * jax.nn.sigmoid(fl_ref[rs, :])
            kk = 1.0 - f
            L = jnp.log(f)
            for sh in steps:
                L = L + jnp.where(pos >= sh, pltpu.roll(L, sh, 0), 0.0)
            Tt = jnp.where(pos == HGRN_BLOCK - 1, L, 0.0)
            for sh in steps:
                Tt = Tt + pltpu.roll(Tt, C - sh, 0)
            a = lax.dot_general((qv * jnp.exp(L)).astype(BF16), (kk * jnp.exp(-L)).astype(BF16),
                                _NT, preferred_element_type=F32)
            A = jnp.where(code == 1, a, 0.0)
            level, R = 2, HGRN_BLOCK
            while R < C:
                upper = jnp.bitwise_and(row, R) != 0
                x = (jnp.where(upper, qv, kk)
                     * jnp.exp(jnp.where(upper, L, Tt - L))).astype(BF16)
                a = lax.dot_general(x, x, _NT, preferred_element_type=F32)
                A = jnp.where(code == level, a, A)
                below = pltpu.roll(Tt, R, 0)
                above = pltpu.roll(Tt, C - R, 0)
                L = L + jnp.where(upper, below, 0.0)
                Tt = Tt + jnp.where(upper, below, above)
                level, R = level + 1, 2 * R
            o = jnp.dot(A.astype(BF16), vv, preferred_element_type=F32)
            o = o + lax.dot_general((qv * jnp.exp(L)).astype(BF16), st.astype(BF16), _NT,
                                    preferred_element_type=F32)
            k_end = (kk * jnp.exp(Tt - L)).astype(BF16)
            st = st * jnp.exp(Tt[0:1, :]) + lax.dot_general(vv, k_end, _TN,
                                                            preferred_element_type=F32)
            o = _rms(o, gnv) * jax.nn.sigmoid(g_ref[rs, :])
            o_ref[rs, :] = o.astype(o_ref.dtype)
        st_ref[...] = st

        @pl.when(t == n_t - 1)
        def _():
            sf_ref[...] = st.T

    blk = pl.BlockSpec((tc, LANES), lambda b, h, t: (b * n_t + t, h))
    vec = pl.BlockSpec((1, LANES), lambda b, h, t: (0, h))
    st_spec = pl.BlockSpec((None, None, dk, dv), lambda b, h, t: (b, h, 0, 0))
    return pl.pallas_call(
        kernel,
        grid=(B, H, n_t),
        in_specs=[blk, blk, blk, blk, vec, vec, st_spec,
                  pl.BlockSpec((C, C), lambda b, h, t: (0, 0))],
        out_specs=[blk, st_spec],
        out_shape=[jax.ShapeDtypeStruct((M, D), BF16), jax.ShapeDtypeStruct(s0.shape, F32)],
        scratch_shapes=[pltpu.VMEM((dv, dk), F32)],
        compiler_params=_params("arbitrary", "arbitrary", "arbitrary"),
        name="hgrn_scan",
    )(q, fl, iv, g, lb.reshape(1, D).astype(F32), gn.reshape(1, D).astype(F32), s0,
      _hgrn_level_codes(C))


def _hgrn_layer(h, gamma, wts, lb, s0, *, B, T):
    w_parts, gn, wo = wts
    M, D = h.shape
    tm = _tile(M, 512, 2 * SUBLANES)

    def cast_epilogue(j, acc, ctx):
        ctx["outs"][0][...] = acc.astype(ctx["outs"][0].dtype)

    parts = []
    for w, dt, nm in zip(w_parts, (F32, F32, BF16, F32), ("q", "f", "i", "g")):
        parts.append(_proj(h, w, tm=tm, gamma=gamma, outs=[_row_out(M, D, tm, dt)],
                           epilogue=cast_epilogue, name="hgrn_%s_proj" % nm)[0])
    C = LANES
    while T % C:
        C //= 2
    o, s_new = _hgrn_scan(*parts, lb, gn, s0, B=B, T=T, tc=_tile(T, 1024, C), C=C)
    h = _out_proj(o, wo, h, T=T, name="hgrn_out_proj")
    return h, s_new


def _conv_ln_silu(u, buf, w_dw, b_dw, ln_g, ln_b, *, B, T, tc):
    M, D = u.shape
    W = w_dw.shape[0]
    halo = buf.shape[1]
    off = halo - (W - 1)
    n_t = T // tc
    assert T % tc == 0 and tc >= halo and off >= 0
    cw = _tile(D, 4 * LANES, LANES)
    n_groups = -(-W // SUBLANES)

    def kernel(u_ref, buf_ref, w_ref, bd_ref, lg_ref, lbias_ref, y_ref, ext_ref, acc_ref,
               sh_ref):
        @pl.when(pl.program_id(1) == 0)
        def _():
            ext_ref[0:halo, :] = buf_ref[...]
        ext_ref[halo:halo + tc, :] = u_ref[...]
        for cb in range(D // cw):
            cols = slice(cb * cw, (cb + 1) * cw)
            part = jnp.zeros((tc, cw), F32) + bd_ref[:, cols]
            for r in range(min(SUBLANES, W)):
                taps = range(r, W, SUBLANES)
                span = tc + (len(taps) - 1) * SUBLANES
                sh_ref[0:span, :] = ext_ref[off + r:off + r + span, cols]
                for m, j in enumerate(taps):
                    part = part + (w_ref[j:j + 1, cols]
                                   * sh_ref[m * SUBLANES:m * SUBLANES + tc, :])
            acc_ref[:, cols] = part
        acc = acc_ref[...]
        mu = jnp.mean(acc, axis=-1, keepdims=True)
        d = acc - mu
        var = jnp.mean(d * d, axis=-1, keepdims=True)
        y = d * lax.rsqrt(var + EPS) * lg_ref[...] + lbias_ref[...]
        y_ref[...] = (y * jax.nn.sigmoid(y)).astype(y_ref.dtype)
        ext_ref[0:halo, :] = ext_ref[tc:tc + halo, :]

    vec = pl.BlockSpec((1, D), lambda b, t: (0, 0))
    return pl.pallas_call(
        kernel,
        grid=(B, n_t),
        in_specs=[pl.BlockSpec((tc, D), lambda b, t: (b * n_t + t, 0)),
                  pl.BlockSpec((None, halo, D), lambda b, t: (b, 0, 0)),
                  pl.BlockSpec((W, D), lambda b, t: (0, 0)), vec, vec, vec],
        out_specs=pl.BlockSpec((tc, D), lambda b, t: (b * n_t + t, 0)),
        out_shape=jax.ShapeDtypeStruct((M, D), BF16),
        scratch_shapes=[pltpu.VMEM((halo + tc, D), F32), pltpu.VMEM((tc, D), F32),
                        pltpu.VMEM((tc + (n_groups - 1) * SUBLANES, cw), F32)],
        compiler_params=_params("arbitrary", "arbitrary"),
        name="conv_ln_silu",
    )(u, buf, w_dw.astype(F32), b_dw.reshape(1, D).astype(F32),
      ln_g.reshape(1, D).astype(F32), ln_b.reshape(1, D).astype(F32))


def _conv_layer(h, gamma, wts, buf, *, B, T):
    w_glu, b_glu, w_dw, b_dw, ln_g, ln_b, w_pw2, b_pw2 = wts
    M, D = h.shape
    W = w_dw.shape[0]
    assert T >= W - 1
    tm = _tile(M, 512)
    wt = w_glu.shape[1] // 2

    def glu_epilogue(j, acc, ctx):
        half = acc.shape[1] // 2
        ctx["outs"][0][...] = acc[:, :half] * jax.nn.sigmoid(acc[:, half:])

    u = _proj(h, w_glu, tm=tm, wt=wt, gamma=gamma, bias=b_glu,
              outs=[_row_out(M, D, tm, F32, wt=wt // 2)], epilogue=glu_epilogue,
              name="conv_glu_proj")[0]
    halo = -(-(W - 1) // SUBLANES) * SUBLANES
    buf_p = jnp.pad(buf.astype(F32), ((0, 0), (halo - (W - 1), 0), (0, 0)))
    y = _conv_ln_silu(u, buf_p, w_dw, b_dw, ln_g, ln_b, B=B, T=T, tc=_tile(T, 256))
    h = _out_proj(y, w_pw2, h, T=T, bias=b_pw2, name="conv_out_proj")
    return h, u.reshape(B, T, D)[:, T - (W - 1):, :]


def _glu_weights(w, b, n_tiles=2):
    K, N2 = w.shape
    tn = N2 // 2 // n_tiles
    wr = w.reshape(K, 2, n_tiles, tn).transpose(0, 2, 1, 3).reshape(K, N2)
    br = b.reshape(2, n_tiles, tn).transpose(1, 0, 2).reshape(N2)
    return wr, br


def kernel(x_prompt, x_sample, p_prompt, p_sample, cache_fox_k, cache_fox_v, cache_fox_logf, state_hgrn, cache_conv, norm_mix, norm_ffn, norm_ple, norm_final, w_in_a, b_f_a, w_o_a, w_in_b, lb_logits, gnorm_b, w_o_b, w_pw1, b_pw1, w_dw, b_dw, ln_g, ln_b, w_pw2, b_pw2, w_up, w_down, w_ple, w_pg):
    depth, D = norm_mix.shape
    n_mixers = 3
    lb_soft = jax.nn.softmax(lb_logits.astype(F32), axis=0)
    lower_bounds = jnp.cumsum(lb_soft, axis=0) - lb_soft[0]

    groups = []
    for x, p in ((x_prompt, p_prompt), (x_sample, p_sample)):
        B, T, _ = x.shape
        groups.append(dict(B=B, T=T, h=x.reshape(B * T, D), p=p.reshape(depth, B * T, -1)))
    is_sample = (False, True)

    fox_k, fox_v, fox_l = [[], []], [[], []], ([], [])
    hg_state, conv_buf = ([], []), ([], [])
    n_fox = len(range(0, depth, n_mixers))
    ia = ib = ic = 0
    for i in range(depth):
        kind = i % n_mixers
        if kind == 0:
            w = w_in_a[ia].astype(BF16)
            wts = (w[:, :D], w[:, D:2 * D], w[:, 2 * D:3 * D], w[:, 3 * D:], b_f_a[ia],
                   w_o_a[ia].astype(BF16))
        elif kind == 1:
            w = w_in_b[ib].astype(BF16)
            wts = (tuple(w[:, n * D:(n + 1) * D] for n in range(4)), gnorm_b[ib],
                   w_o_b[ib].astype(BF16))
        else:
            wts = _glu_weights(w_pw1[ic].astype(BF16), b_pw1[ic]) + (
                w_dw[ic], b_dw[ic], ln_g[ic], ln_b[ic], w_pw2[ic].astype(BF16), b_pw2[ic])
        wu, wd = w_up[i].astype(BF16), w_down[i].astype(BF16)
        wple, wpg = w_ple[i].astype(BF16), w_pg[i].astype(BF16)
        for gi, grp in enumerate(groups):
            B, T, h = grp["B"], grp["T"], grp["h"]
            if kind == 0:
                cache = ((cache_fox_k[ia], cache_fox_v[ia], cache_fox_logf[ia])
                         if is_sample[gi] else None)
                stack = ia == n_fox - 1
                earlier = (tuple(fox_k[gi]), tuple(fox_v[gi])) if stack else ((), ())
                h, k_new, v_new, l_new = _fox_layer(h, norm_mix[i], wts, earlier, stack,
                                                    B=B, T=T, cache=cache)
                if stack:
                    fox_k[gi], fox_v[gi] = k_new, v_new
                else:
                    fox_k[gi].append(k_new)
                    fox_v[gi].append(v_new)
                fox_l[gi].append(l_new)
            elif kind == 1:
                s0 = (state_hgrn[ib].astype(F32) if is_sample[gi]
                      else jnp.zeros((B,) + state_hgrn.shape[2:], F32))
                h, s_new = _hgrn_layer(h, norm_mix[i], wts, lower_bounds[i], s0, B=B, T=T)
                hg_state[gi].append(s_new)
            else:
                buf = (cache_conv[ic] if is_sample[gi]
                       else jnp.zeros((B, w_dw.shape[1] - 1, D), F32))
                h, b_new = _conv_layer(h, norm_mix[i], wts, buf, B=B, T=T)
                conv_buf[gi].append(b_new)
            h = _ffn(h, norm_ffn[i], wu, wd)
            h = _ple(h, grp["p"], i, norm_ple[i], wple, wpg,
                     final_gamma=norm_final if i == depth - 1 else None)
            grp["h"] = h
        if kind == 0:
            ia += 1
        elif kind == 1:
            ib += 1
        else:
            ic += 1

    ys = [grp["h"].reshape(grp["B"], grp["T"], D) for grp in groups]
    return (ys[0], ys[1],
            fox_k[0], fox_v[0], jnp.stack(fox_l[0]),
            fox_k[1], fox_v[1], jnp.stack(fox_l[1]),
            jnp.stack(hg_state[0]), jnp.stack(hg_state[1]),
            jnp.stack(conv_buf[0]), jnp.stack(conv_buf[1]))
```

```python
import functools

import jax
import jax.numpy as jnp
import numpy as np
from jax import lax
from jax.experimental import pallas as pl
from jax.experimental.pallas import tpu as pltpu

F32 = jnp.float32
BF16 = jnp.bfloat16
EPS = 1e-6
NEG_INF = -1e30
LOG2E = 1.4426950408889634
HGRN_BLOCK = 16
LANES = 128
SUBLANES = 8
VMEM_LIMIT_BYTES = 56 * 1024 * 1024

_NT = (((1,), (1,)), ((), ()))
_TN = (((0,), (0,)), ((), ()))


def _tile(n, pref, mult=SUBLANES):
    for d in range(min(n, pref), 0, -1):
        if n % d == 0 and d % mult == 0:
            return d
    return n


def _params(*sem):
    return pltpu.CompilerParams(dimension_semantics=sem, vmem_limit_bytes=VMEM_LIMIT_BYTES)


def _rms(x, g):
    return x * lax.rsqrt(jnp.mean(x * x, axis=-1, keepdims=True) + EPS) * g


def _split3(x):
    hi = x.astype(BF16)
    r1 = x - hi.astype(F32)
    mid = r1.astype(BF16)
    lo = (r1 - mid.astype(F32)).astype(BF16)
    return hi, mid, lo


def _dot_ones_rhs(x, mask_bf16):
    out = None
    for part in _split3(x):
        d = jnp.dot(part, mask_bf16, preferred_element_type=F32)
        out = d if out is None else out + d
    return out


def _proj(x, w, *, tm, wt=None, gamma=None, bias=None, head_major_x=False,
          extras=(), outs, epilogue, name):
    K, N = w.shape
    wt = N if wt is None else wt
    if head_major_x:
        B, H, T, dh = x.shape
        M = B * T
        n_t = T // tm
        x_spec = pl.BlockSpec((None, H, tm, dh), lambda i, j: (i // n_t, 0, i % n_t, 0))
    else:
        M = x.shape[0]
        x_spec = pl.BlockSpec((tm, K), lambda i, j: (i, 0))
    has_g = gamma is not None
    has_b = bias is not None
    assert M % tm == 0 and N % wt == 0
    assert has_g == (x.dtype == F32)

    def kernel(*refs):
        it = iter(refs)
        x_ref = next(it)
        g_ref = next(it) if has_g else None
        w_ref = next(it)
        b_ref = next(it) if has_b else None
        extra_refs = [next(it) for _ in extras]
        out_refs = [next(it) for _ in outs]
        xn_ref = next(it) if has_g else None
        j = pl.program_id(1)
        if has_g:
            @pl.when(j == 0)
            def _():
                xn_ref[...] = _rms(x_ref[...], g_ref[...]).astype(BF16)
            lhs = xn_ref[...]
        elif head_major_x:
            lhs = jnp.concatenate([x_ref[h] for h in range(x_ref.shape[0])], axis=1)
        else:
            lhs = x_ref[...]
        acc = jnp.dot(lhs, w_ref[...], preferred_element_type=F32)
        if has_b:
            acc = acc + b_ref[...]
        epilogue(j, acc, dict(x=x_ref, lhs=lhs, extras=extra_refs, outs=out_refs))

    in_arrays = [x]
    in_specs = [x_spec]
    if has_g:
        in_arrays.append(gamma.reshape(1, K).astype(F32))
        in_specs.append(pl.BlockSpec((1, K), lambda i, j: (0, 0)))
    in_arrays.append(w)
    in_specs.append(pl.BlockSpec((K, wt), lambda i, j: (0, j)))
    if has_b:
        in_arrays.append(bias.reshape(1, N).astype(F32))
        in_specs.append(pl.BlockSpec((1, wt), lambda i, j: (0, j)))
    for arr, blk, imap in extras:
        in_arrays.append(arr)
        in_specs.append(pl.BlockSpec(blk, imap))
    return pl.pallas_call(
        kernel,
        grid=(M // tm, N // wt),
        in_specs=in_specs,
        out_specs=[pl.BlockSpec(blk, imap) for _, blk, imap in outs],
        out_shape=[sds for sds, _, _ in outs],
        scratch_shapes=[pltpu.VMEM((tm, K), BF16)] if has_g else [],
        compiler_params=_params("arbitrary", "arbitrary"),
        name=name,
    )(*in_arrays)


def _store_head_major(ref, acc):
    for h in range(ref.shape[0]):
        ref[h] = acc[:, h * ref.shape[2]:(h + 1) * ref.shape[2]].astype(ref.dtype)


def _head_major_out(B, H, T, dh, tm):
    n_t = T // tm
    return (jax.ShapeDtypeStruct((B, H, T, dh), BF16), (None, H, tm, dh),
            lambda i, j: (i // n_t, 0, i % n_t, 0))


def _row_out(M, N, tm, dtype, wt=None):
    wt = N if wt is None else wt
    return (jax.ShapeDtypeStruct((M, N), dtype), (tm, wt), lambda i, j: (i, j))


def _log_sigmoid(x):
    return jnp.minimum(x, 0.0) - jnp.log(1.0 + jnp.exp(-jnp.abs(x)))


def _fox_project(h, gamma, wq, wk, wv, wf, bf, earlier, stack, *, B, T, H):
    M, D = h.shape
    dh = D // H
    tm = _tile(T, 512)
    scale = dh ** -0.5 * LOG2E

    def q_epilogue(j, acc, ctx):
        q_ref, logf_ref = ctx["outs"]
        wf_ref, bf_ref = ctx["extras"]
        _store_head_major(q_ref, acc * scale)
        fl = jnp.dot(ctx["lhs"], wf_ref[...], preferred_element_type=F32)
        logf_ref[...] = _log_sigmoid(fl[:, :H] + bf_ref[...])

    wf_pad = jnp.pad(wf, ((0, 0), (0, LANES - H)))
    q_hm, logf = _proj(
        h, wq, tm=tm, gamma=gamma,
        extras=[(wf_pad, (D, LANES), lambda i, j: (0, 0)),
                (bf.reshape(1, H).astype(F32), (1, H), lambda i, j: (0, 0))],
        outs=[_head_major_out(B, H, T, dh, tm), _row_out(M, H, tm, F32)],
        epilogue=q_epilogue, name="fox_q_proj")

    k32, k_hm = _fox_kv_proj(h, gamma, wk, earlier[0], stack, B=B, T=T, H=H, tm=tm,
                             name="fox_k_proj")
    v32, v_hm = _fox_kv_proj(h, gamma, wv, earlier[1], stack, B=B, T=T, H=H, tm=tm,
                             name="fox_v_proj")
    return q_hm, k32, k_hm, v32, v_hm, logf


def _fox_kv_proj(h, gamma, w, earlier, stack, *, B, T, H, tm, name):
    M, D = h.shape
    dh = D // H
    n_t = T // tm
    n = M // tm
    n_e = len(earlier)
    assert stack or not n_e

    def kernel(x_ref, g_ref, w_ref, *rest):
        earlier_refs = rest[:n_e]
        full_ref, hm_ref, stage_ref, sem_ref, esem_ref = rest[n_e:]
        own = full_ref.at[n_e] if stack else full_ref
        i = pl.program_id(0)
        slot = i % 2

        def head_copy(step, slot, h):
            t0 = pl.multiple_of((step % n_t) * tm, tm)
            return pltpu.make_async_copy(stage_ref.at[slot, :, pl.ds(h * dh, dh)],
                                         own.at[step // n_t, pl.ds(t0, tm), h, :],
                                         sem_ref.at[slot])

        def earlier_copy(e):
            t0 = pl.multiple_of((i % n_t) * tm, tm)
            return pltpu.make_async_copy(earlier_refs[e],
                                         full_ref.at[e, pl.ds(i // n_t, 1), pl.ds(t0, tm)],
                                         esem_ref.at[e])

        for e in range(n_e):
            earlier_copy(e).start()

        @pl.when(i >= 2)
        def _():
            for hd in range(H):
                head_copy(i - 2, slot, hd).wait()

        acc = jnp.dot(_rms(x_ref[...], g_ref[...]).astype(BF16), w_ref[...],
                      preferred_element_type=F32)
        stage_ref[slot] = acc
        _store_head_major(hm_ref, acc)
        for hd in range(H):
            head_copy(i, slot, hd).start()
        for e in range(n_e):
            earlier_copy(e).wait()

        @pl.when(i == n - 1)
        def _():
            if n >= 2:
                for hd in range(H):
                    head_copy(i - 1, 1 - slot, hd).wait()
            for hd in range(H):
                head_copy(i, slot, hd).wait()

    shape = (n_e + 1, B, T, H, dh) if stack else (B, T, H, dh)
    return pl.pallas_call(
        kernel,
        grid=(n,),
        in_specs=[pl.BlockSpec((tm, D), lambda i: (i, 0)),
                  pl.BlockSpec((1, D), lambda i: (0, 0)),
                  pl.BlockSpec((D, D), lambda i: (0, 0))]
        + [pl.BlockSpec((1, tm, H, dh), lambda i: (i // n_t, i % n_t, 0, 0))] * n_e,
        out_specs=[pl.BlockSpec(memory_space=pl.ANY),
                   pl.BlockSpec((None, H, tm, dh), lambda i: (i // n_t, 0, i % n_t, 0))],
        out_shape=[jax.ShapeDtypeStruct(shape, F32), jax.ShapeDtypeStruct((B, H, T, dh), BF16)],
        scratch_shapes=[pltpu.VMEM((2, tm, D), F32), pltpu.SemaphoreType.DMA((2,)),
                        pltpu.SemaphoreType.DMA((max(n_e, 1),))],
        compiler_params=_params("arbitrary"),
        name=name,
    )(h, gamma.reshape(1, D).astype(F32), w, *earlier)


def _cumsum_lanes(x):
    R, L = x.shape
    assert L % LANES == 0

    def kernel(x_ref, o_ref):
        upper = (lax.broadcasted_iota(jnp.int32, (LANES, LANES), 0)
                 <= lax.broadcasted_iota(jnp.int32, (LANES, LANES), 1)).astype(BF16)
        carry = jnp.zeros((R, 1), F32)
        for c in range(L // LANES):
            sl = slice(c * LANES, (c + 1) * LANES)
            loc = _dot_ones_rhs(x_ref[:, sl], upper)
            o_ref[:, sl] = loc + carry
            carry = carry + loc[:, LANES - 1:LANES]

    return pl.pallas_call(
        kernel,
        out_shape=jax.ShapeDtypeStruct((R, L), F32),
        compiler_params=pltpu.CompilerParams(vmem_limit_bytes=VMEM_LIMIT_BYTES),
        name="fox_cumsum",
    )(x)


def _fox_attention(q, k, v, cq, ck, *, q_off, tq, tk):
    B, H, Tq, dh = q.shape
    Tk = k.shape[2]
    nq, nk = Tq // tq, Tk // tk
    assert Tq % tq == 0 and Tk % tk == 0 and tk % LANES == 0 and dh == LANES
    n_c = tk // LANES
    unroll = next(u for u in (16, 8, 4, 2, 1) if H % u == 0)

    pairs = [(qi, ki) for qi in range(nq)
             for ki in range(min(nk - 1, (q_off + (qi + 1) * tq - 1) // tk) + 1)]

    def kernel(qt_ref, kt_ref, lt_ref, q_ref, k_ref, v_ref, cq_ref, ck_ref, o_ref, m_ref, acc_ref,
               cqs_ref):
        n = pl.program_id(1)
        qi = qt_ref[n]
        ki = kt_ref[n]

        @pl.when(ki == 0)
        def _():
            m_ref[...] = jnp.full(m_ref.shape, NEG_INF, F32)
            acc_ref[...] = jnp.zeros(acc_ref.shape, F32)
            cqv = cq_ref[...] * LOG2E
            for h in range(H):
                cqs_ref[h] = jnp.broadcast_to(cqv[:, h:h + 1], (tq, LANES))

        q_lo = q_off + qi * tq
        k_lo = ki * tk
        crosses = k_lo + tk - 1 > q_lo

        def one_head(h, masked):
            s = lax.dot_general(q_ref[h], k_ref[h], _NT, preferred_element_type=F32)
            cqh = cqs_ref[h]
            ckh = ck_ref[pl.ds(h, 1), :] * LOG2E
            if masked:
                rel = (lax.broadcasted_iota(jnp.int32, (tq, LANES), 1)
                       - lax.broadcasted_iota(jnp.int32, (tq, LANES), 0))
            chunks = []
            for c in range(n_c):
                sl = slice(c * LANES, (c + 1) * LANES)
                sc = s[:, sl] - ckh[:, sl]
                if masked:
                    sc = jnp.where(rel <= q_lo - k_lo - c * LANES, sc, NEG_INF)
                chunks.append(sc)
            mx = functools.reduce(jnp.maximum, chunks)
            m_prev = m_ref[h]
            m_new = jnp.maximum(m_prev, jnp.max(mx, axis=1, keepdims=True) + cqh)
            alpha = jnp.exp2(m_prev - m_new)
            shift = m_new - cqh
            p = jnp.concatenate([jnp.exp2(sc - shift).astype(BF16) for sc in chunks], axis=1)
            v_ones = jnp.concatenate([v_ref[h], jnp.ones((tk, dh), BF16)], axis=1)
            pv = jnp.dot(p, v_ones, preferred_element_type=F32)
            acc_ref[h] = acc_ref[h] * jnp.concatenate([alpha, alpha], axis=1) + pv
            m_ref[h] = m_new

        def heads(masked):
            def body(i, carry):
                for u in range(unroll):
                    one_head(i * unroll + u, masked)
                return carry
            lax.fori_loop(0, H // unroll, body, 0)

        @pl.when(crosses)
        def _():
            heads(True)

        @pl.when(jnp.logical_not(crosses))
        def _():
            heads(False)

        @pl.when(lt_ref[n] == 1)
        def _():
            a = acc_ref[...]
            o_ref[...] = (a[:, :, :dh] / a[:, :, dh:]).astype(o_ref.dtype)

    q_spec = pl.BlockSpec((None, H, tq, dh), lambda b, n, qt, kt, lt: (b, 0, qt[n], 0))
    kv_spec = pl.BlockSpec((None, H, tk, dh), lambda b, n, qt, kt, lt: (b, 0, kt[n], 0))
    grid_spec = pltpu.PrefetchScalarGridSpec(
        num_scalar_prefetch=3,
        grid=(B, len(pairs)),
        in_specs=[q_spec, kv_spec, kv_spec,
                  pl.BlockSpec((None, tq, H), lambda b, n, qt, kt, lt: (b, qt[n], 0)),
                  pl.BlockSpec((None, H, tk), lambda b, n, qt, kt, lt: (b, 0, kt[n]))],
        out_specs=q_spec,
        scratch_shapes=[pltpu.VMEM((H, tq, LANES), F32), pltpu.VMEM((H, tq, 2 * dh), F32),
                        pltpu.VMEM((H, tq, LANES), F32)])
    return pl.pallas_call(
        kernel,
        grid_spec=grid_spec,
        out_shape=jax.ShapeDtypeStruct((B, H, Tq, dh), BF16),
        compiler_params=_params("arbitrary", "arbitrary"),
        name="fox_attention",
    )(jnp.asarray([p[0] for p in pairs], jnp.int32), jnp.asarray([p[1] for p in pairs], jnp.int32),
      jnp.asarray([n + 1 == len(pairs) or pairs[n + 1][0] != p[0] for n, p in enumerate(pairs)],
                  jnp.int32),
      q, k, v, cq, ck)


def _to_head_major(x):
    return jnp.transpose(x, (0, 2, 1, 3)).astype(BF16)


def _fox_layer(h, gamma, wts, earlier, stack, *, B, T, cache=None):
    wq, wk, wv, wf, bf, wo = wts
    M, D = h.shape
    H = bf.shape[0]
    q_hm, k32, k_hm, v32, v_hm, logf = _fox_project(h, gamma, wq, wk, wv, wf, bf, earlier, stack,
                                                    B=B, T=T, H=H)
    logf_bth = logf.reshape(B, T, H)
    if cache is None:
        P = 0
        logf_all = logf_bth
    else:
        ck_, cv_, cl_ = cache
        P = ck_.shape[1]
        k_hm = jnp.concatenate([_to_head_major(ck_), k_hm], axis=2)
        v_hm = jnp.concatenate([_to_head_major(cv_), v_hm], axis=2)
        logf_all = jnp.concatenate([cl_.astype(F32), logf_bth], axis=1)
    Tk = P + T
    tk = _tile(Tk, 512, LANES) if Tk % LANES == 0 else -(-Tk // LANES) * LANES
    pad = -Tk % tk
    if pad:
        k_hm = jnp.pad(k_hm, ((0, 0), (0, 0), (0, pad), (0, 0)))
        v_hm = jnp.pad(v_hm, ((0, 0), (0, 0), (0, pad), (0, 0)))
        logf_all = jnp.pad(logf_all, ((0, 0), (0, pad), (0, 0)))
    rows = jnp.transpose(logf_all, (0, 2, 1)).reshape(B * H, Tk + pad)
    ck = _cumsum_lanes(rows).reshape(B, H, Tk + pad)
    cq = jnp.transpose(ck[:, :, P:P + T], (0, 2, 1))
    o_hm = _fox_attention(q_hm, k_hm, v_hm, cq, ck, q_off=P, tq=_tile(T, 512), tk=tk)
    h = _out_proj(o_hm, wo, h, T=T, head_major=True, name="fox_out_proj")
    return h, k32, v32, logf_bth


def _out_proj(x, w, h, *, T, bias=None, head_major=False, name):
    M, D = h.shape
    tm = _tile(T if head_major else M, 512, 2 * SUBLANES)

    def epilogue(j, acc, ctx):
        ctx["outs"][0][...] = ctx["extras"][0][...] + acc

    return _proj(x, w, tm=tm, bias=bias, head_major_x=head_major,
                 extras=[(h, (tm, D), lambda i, j: (i, 0))],
                 outs=[_row_out(M, D, tm, F32)], epilogue=epilogue, name=name)[0]


def _ffn(h, gamma, w_up, w_down, layer):
    M, D = h.shape
    F = w_up.shape[2]
    tm = _tile(M, 512)
    tf = _tile(F, 1024, LANES)

    def kernel(x_ref, g_ref, wu_ref, wd_ref, o_ref, xn_ref):
        @pl.when(pl.program_id(1) == 0)
        def _():
            x = x_ref[...]
            xn_ref[...] = _rms(x, g_ref[...]).astype(BF16)
            o_ref[...] = x
        hid = jnp.dot(xn_ref[...], wu_ref[...], preferred_element_type=F32)
        hid = jnp.square(jnp.maximum(hid, 0.0)).astype(BF16)
        o_ref[...] += jnp.dot(hid, wd_ref[...], preferred_element_type=F32)

    return pl.pallas_call(
        kernel,
        grid=(M // tm, F // tf),
        in_specs=[pl.BlockSpec((tm, D), lambda i, f: (i, 0)),
                  pl.BlockSpec((1, D), lambda i, f: (0, 0)),
                  pl.BlockSpec((None, D, tf), lambda i, f: (layer, 0, f)),
                  pl.BlockSpec((None, tf, D), lambda i, f: (layer, f, 0))],
        out_specs=pl.BlockSpec((tm, D), lambda i, f: (i, 0)),
        out_shape=jax.ShapeDtypeStruct((M, D), F32),
        scratch_shapes=[pltpu.VMEM((tm, D), BF16)],
        compiler_params=_params("arbitrary", "arbitrary"),
        name="sq_relu_mlp",
    )(h, gamma.reshape(1, D).astype(F32), w_up, w_down)


def _ple(h, p, layer, gamma, w_ple, w_pg, *, final_gamma=None):
    M, D = h.shape
    E = p.shape[2]
    tm = _tile(M, 512)
    fin = final_gamma is not None

    def epilogue(j, acc, ctx):
        p_ref, wple_ref = ctx["extras"][:2]
        e = jnp.dot(p_ref[...].astype(BF16), wple_ref[...], preferred_element_type=F32)
        out = ctx["x"][...] + e * jax.nn.sigmoid(acc)
        if fin:
            out = _rms(out, ctx["extras"][2][...])
        ctx["outs"][0][...] = out

    extras = [(p, (None, tm, E), lambda i, j: (layer, i, 0)),
              (w_ple, (E, D), lambda i, j: (0, 0))]
    if fin:
        extras.append((final_gamma.reshape(1, D).astype(F32), (1, D), lambda i, j: (0, 0)))
    return _proj(h, w_pg, tm=tm, gamma=gamma, extras=extras,
                 outs=[_row_out(M, D, tm, F32)], epilogue=epilogue, name="ple_add")[0]


def _hgrn_level_codes(C):
    t = np.arange(C)[:, None]
    s = np.arange(C)[None, :]
    code = np.where((t // HGRN_BLOCK == s // HGRN_BLOCK) & (s <= t), 1, 0)
    level, R = 2, HGRN_BLOCK
    while R < C:
        code = np.where(((t // R) % 2 == 1) & (s // R == t // R - 1), level, code)
        level, R = level + 1, 2 * R
    return jnp.asarray(code, jnp.int32)


def _hgrn_scan(q, fl, iv, g, lb, gn, s0, *, B, T, tc, C):
    M, D = q.shape
    H, dk, dv = s0.shape[1:]
    assert dk == LANES and dv == LANES and T % tc == 0 and tc % C == 0
    assert C % HGRN_BLOCK == 0 and (C // HGRN_BLOCK) & (C // HGRN_BLOCK - 1) == 0
    n_t = T // tc
    steps = [1 << i for i in range(HGRN_BLOCK.bit_length() - 1)]

    def kernel(q_ref, fl_ref, iv_ref, g_ref, lb_ref, gn_ref, s0_ref, code_ref, o_ref, sf_ref,
               st_ref):
        t = pl.program_id(2)

        @pl.when(t == 0)
        def _():
            st_ref[...] = s0_ref[...].T

        lbv = lb_ref[...]
        gnv = gn_ref[...]
        code = code_ref[...]
        row = lax.broadcasted_iota(jnp.int32, (C, LANES), 0)
        pos = jnp.bitwise_and(row, HGRN_BLOCK - 1)
        st = st_ref[...]
        for c in range(tc // C):
            rs = slice(c * C, (c + 1) * C)
            qv = q_ref[rs, :]
            vv = iv_ref[rs, :]
            f = lbv + (1.0 - lbv) * jax.nn.sigmoid(fl_ref[rs, :])
            kk = 1.0 - f
            L = jnp.log(f)
            for sh in steps:
                L = L + jnp.where(pos >= sh, pltpu.roll(L, sh, 0), 0.0)
            Tt = jnp.where(pos == HGRN_BLOCK - 1, L, 0.0)
            for sh in steps:
                Tt = Tt + pltpu.roll(Tt, C - sh, 0)
            a = lax.dot_general((qv * jnp.exp(L)).astype(BF16), (kk * jnp.exp(-L)).astype(BF16),
                                _NT, preferred_element_type=F32)
            A = jnp.where(code == 1, a, 0.0)
            level, R = 2, HGRN_BLOCK
            while R < C:
                upper = jnp.bitwise_and(row, R) != 0
                x = (jnp.where(upper, qv, kk)
                     * jnp.exp(jnp.where(upper, L, Tt - L))).astype(BF16)
                a = lax.dot_general(x, x, _NT, preferred_element_type=F32)
                A = jnp.where(code == level, a, A)
                below = pltpu.roll(Tt, R, 0)
                above = pltpu.roll(Tt, C - R, 0)
                L = L + jnp.where(upper, below, 0.0)
                Tt = Tt + jnp.where(upper, below, above)
                level, R = level + 1, 2 * R
            o = jnp.dot(A.astype(BF16), vv, preferred_element_type=F32)
            o = o + lax.dot_general((qv * jnp.exp(L)).astype(BF16), st.astype(BF16), _NT,
                                    preferred_element_type=F32)
            k_end = (kk * jnp.exp(Tt - L)).astype(BF16)
            st = st * jnp.exp(Tt[0:1, :]) + lax.dot_general(vv, k_end, _TN,
                                                            preferred_element_type=F32)
            o = _rms(o, gnv) * jax.nn.sigmoid(g_ref[rs, :])
            o_ref[rs, :] = o.astype(o_ref.dtype)
        st_ref[...] = st

        @pl.when(t == n_t - 1)
        def _():
            sf_ref[...] = st.T

    blk = pl.BlockSpec((tc, LANES), lambda b, h, t: (b * n_t + t, h))
    vec = pl.BlockSpec((1, LANES), lambda b, h, t: (0, h))
    st_spec = pl.BlockSpec((None, None, dk, dv), lambda b, h, t: (b, h, 0, 0))
    return pl.pallas_call(
        kernel,
        grid=(B, H, n_t),
        in_specs=[blk, blk, blk, blk, vec, vec, st_spec,
                  pl.BlockSpec((C, C), lambda b, h, t: (0, 0))],
        out_specs=[blk, st_spec],
        out_shape=[jax.ShapeDtypeStruct((M, D), BF16), jax.ShapeDtypeStruct(s0.shape, F32)],
        scratch_shapes=[pltpu.VMEM((dv, dk), F32)],
        compiler_params=_params("arbitrary", "arbitrary", "arbitrary"),
        name="hgrn_scan",
    )(q, fl, iv, g, lb.reshape(1, D).astype(F32), gn.reshape(1, D).astype(F32), s0,
      _hgrn_level_codes(C))


def _hgrn_layer(h, gamma, wts, lb, s0, *, B, T):
    w_parts, gn, wo = wts
    M, D = h.shape
    tm = _tile(M, 512, 2 * SUBLANES)

    def cast_epilogue(j, acc, ctx):
        ctx["outs"][0][...] = acc.astype(ctx["outs"][0].dtype)

    parts = []
    for w, dt, nm in zip(w_parts, (F32, F32, BF16, F32), ("q", "f", "i", "g")):
        parts.append(_proj(h, w, tm=tm, gamma=gamma, outs=[_row_out(M, D, tm, dt)],
                           epilogue=cast_epilogue, name="hgrn_%s_proj" % nm)[0])
    C = LANES
    while T % C:
        C //= 2
    o, s_new = _hgrn_scan(*parts, lb, gn, s0, B=B, T=T, tc=_tile(T, 1024, C), C=C)
    h = _out_proj(o, wo, h, T=T, name="hgrn_out_proj")
    return h, s_new


def _conv_ln_silu(u, buf, w_dw, b_dw, ln_g, ln_b, *, B, T, tc):
    M, D = u.shape
    W = w_dw.shape[0]
    halo = buf.shape[1]
    off = halo - (W - 1)
    n_t = T // tc
    assert T % tc == 0 and tc >= halo and off >= 0
    cw = _tile(D, 4 * LANES, LANES)
    n_groups = -(-W // SUBLANES)

    def kernel(u_ref, buf_ref, w_ref, bd_ref, lg_ref, lbias_ref, y_ref, ext_ref, acc_ref,
               sh_ref):
        @pl.when(pl.program_id(1) == 0)
        def _():
            ext_ref[0:halo, :] = buf_ref[...]
        ext_ref[halo:halo + tc, :] = u_ref[...]
        for cb in range(D // cw):
            cols = slice(cb * cw, (cb + 1) * cw)
            part = jnp.zeros((tc, cw), F32) + bd_ref[:, cols]
            for r in range(min(SUBLANES, W)):
                taps = range(r, W, SUBLANES)
                span = tc + (len(taps) - 1) * SUBLANES
                sh_ref[0:span, :] = ext_ref[off + r:off + r + span, cols]
                for m, j in enumerate(taps):
                    part = part + (w_ref[j:j + 1, cols]
                                   * sh_ref[m * SUBLANES:m * SUBLANES + tc, :])
            acc_ref[:, cols] = part
        acc = acc_ref[...]
        mu = jnp.mean(acc, axis=-1, keepdims=True)
        d = acc - mu
        var = jnp.mean(d * d, axis=-1, keepdims=True)
        y = d * lax.rsqrt(var + EPS) * lg_ref[...] + lbias_ref[...]
        y_ref[...] = (y * jax.nn.sigmoid(y)).astype(y_ref.dtype)
        ext_ref[0:halo, :] = ext_ref[tc:tc + halo, :]

    vec = pl.BlockSpec((1, D), lambda b, t: (0, 0))
    return pl.pallas_call(
        kernel,
        grid=(B, n_t),
        in_specs=[pl.BlockSpec((tc, D), lambda b, t: (b * n_t + t, 0)),
                  pl.BlockSpec((None, halo, D), lambda b, t: (b, 0, 0)),
                  pl.BlockSpec((W, D), lambda b, t: (0, 0)), vec, vec, vec],
        out_specs=pl.BlockSpec((tc, D), lambda b, t: (b * n_t + t, 0)),
        out_shape=jax.ShapeDtypeStruct((M, D), BF16),
        scratch_shapes=[pltpu.VMEM((halo + tc, D), F32), pltpu.VMEM((tc, D), F32),
                        pltpu.VMEM((tc + (n_groups - 1) * SUBLANES, cw), F32)],
        compiler_params=_params("arbitrary", "arbitrary"),
        name="conv_ln_silu",
    )(u, buf, w_dw.astype(F32), b_dw.reshape(1, D).astype(F32),
      ln_g.reshape(1, D).astype(F32), ln_b.reshape(1, D).astype(F32))


def _conv_layer(h, gamma, wts, buf, *, B, T):
    w_glu, b_glu, w_dw, b_dw, ln_g, ln_b, w_pw2, b_pw2 = wts
    M, D = h.shape
    W = w_dw.shape[0]
    assert T >= W - 1
    tm = _tile(M, 512)
    wt = w_glu.shape[1] // 2

    def glu_epilogue(j, acc, ctx):
        half = acc.shape[1] // 2
        ctx["outs"][0][...] = acc[:, :half] * jax.nn.sigmoid(acc[:, half:])

    u = _proj(h, w_glu, tm=tm, wt=wt, gamma=gamma, bias=b_glu,
              outs=[_row_out(M, D, tm, F32, wt=wt // 2)], epilogue=glu_epilogue,
              name="conv_glu_proj")[0]
    halo = -(-(W - 1) // SUBLANES) * SUBLANES
    buf_p = jnp.pad(buf.astype(F32), ((0, 0), (halo - (W - 1), 0), (0, 0)))
    y = _conv_ln_silu(u, buf_p, w_dw, b_dw, ln_g, ln_b, B=B, T=T, tc=_tile(T, 256))
    h = _out_proj(y, w_pw2, h, T=T, bias=b_pw2, name="conv_out_proj")
    return h, u.reshape(B, T, D)[:, T - (W - 1):, :]


def _glu_weights(w, b, n_tiles=2):
    K, N2 = w.shape
    tn = N2 // 2 // n_tiles
    wr = w.reshape(K, 2, n_tiles, tn).transpose(0, 2, 1, 3).reshape(K, N2)
    br = b.reshape(2, n_tiles, tn).transpose(1, 0, 2).reshape(N2)
    return wr, br


def kernel(x_prompt, x_sample, p_prompt, p_sample, cache_fox_k, cache_fox_v, cache_fox_logf, state_hgrn, cache_conv, norm_mix, norm_ffn, norm_ple, norm_final, w_in_a, b_f_a, w_o_a, w_in_b, lb_logits, gnorm_b, w_o_b, w_pw1, b_pw1, w_dw, b_dw, ln_g, ln_b, w_pw2, b_pw2, w_up, w_down, w_ple, w_pg):
    depth, D = norm_mix.shape
    n_mixers = 3
    lb_soft = jax.nn.softmax(lb_logits.astype(F32), axis=0)
    lower_bounds = jnp.cumsum(lb_soft, axis=0) - lb_soft[0]

    groups = []
    for x, p in ((x_prompt, p_prompt), (x_sample, p_sample)):
        B, T, _ = x.shape
        groups.append(dict(B=B, T=T, h=x.reshape(B * T, D), p=p.reshape(depth, B * T, -1)))
    is_sample = (False, True)

    fox_k, fox_v, fox_l = [[], []], [[], []], ([], [])
    hg_state, conv_buf = ([], []), ([], [])
    n_fox = len(range(0, depth, n_mixers))
    w_up_b, w_down_b = w_up.astype(BF16), w_down.astype(BF16)
    ia = ib = ic = 0
    for i in range(depth):
        kind = i % n_mixers
        if kind == 0:
            w = w_in_a[ia].astype(BF16)
            wts = (w[:, :D], w[:, D:2 * D], w[:, 2 * D:3 * D], w[:, 3 * D:], b_f_a[ia],
                   w_o_a[ia].astype(BF16))
        elif kind == 1:
            w = w_in_b[ib].astype(BF16)
            wts = (tuple(w[:, n * D:(n + 1) * D] for n in range(4)), gnorm_b[ib],
                   w_o_b[ib].astype(BF16))
        else:
            wts = _glu_weights(w_pw1[ic].astype(BF16), b_pw1[ic]) + (
                w_dw[ic], b_dw[ic], ln_g[ic], ln_b[ic], w_pw2[ic].astype(BF16), b_pw2[ic])
        wple, wpg = w_ple[i].astype(BF16), w_pg[i].astype(BF16)
        for gi, grp in enumerate(groups):
            B, T, h = grp["B"], grp["T"], grp["h"]
            if kind == 0:
                cache = ((cache_fox_k[ia], cache_fox_v[ia], cache_fox_logf[ia])
                         if is_sample[gi] else None)
                stack = ia == n_fox - 1
                earlier = (tuple(fox_k[gi]), tuple(fox_v[gi])) if stack else ((), ())
                h, k_new, v_new, l_new = _fox_layer(h, norm_mix[i], wts, earlier, stack,
                                                    B=B, T=T, cache=cache)
                if stack:
                    fox_k[gi], fox_v[gi] = k_new, v_new
                else:
                    fox_k[gi].append(k_new)
                    fox_v[gi].append(v_new)
                fox_l[gi].append(l_new)
            elif kind == 1:
                s0 = (state_hgrn[ib].astype(F32) if is_sample[gi]
                      else jnp.zeros((B,) + state_hgrn.shape[2:], F32))
                h, s_new = _hgrn_layer(h, norm_mix[i], wts, lower_bounds[i], s0, B=B, T=T)
                hg_state[gi].append(s_new)
            else:
                buf = (cache_conv[ic] if is_sample[gi]
                       else jnp.zeros((B, w_dw.shape[1] - 1, D), F32))
                h, b_new = _conv_layer(h, norm_mix[i], wts, buf, B=B, T=T)
                conv_buf[gi].append(b_new)
            h = _ffn(h, norm_ffn[i], w_up_b, w_down_b, i)
            h = _ple(h, grp["p"], i, norm_ple[i], wple, wpg,
                     final_gamma=norm_final if i == depth - 1 else None)
            grp["h"] = h
        if kind == 0:
            ia += 1
        elif kind == 1:
            ib += 1
        else:
            ic += 1

    ys = [grp["h"].reshape(grp["B"], grp["T"], D) for grp in groups]
    return (ys[0], ys[1],
            fox_k[0], fox_v[0], jnp.stack(fox_l[0]),
            fox_k[1], fox_v[1], jnp.stack(fox_l[1]),
            jnp.stack(hg_state[0]), jnp.stack(hg_state[1]),
            jnp.stack(conv_buf[0]), jnp.stack(conv_buf[1]))
```
